```python
import math
import jax, jax.numpy as jnp
from jax import lax
import numpy as np

D_MODEL = 1024
BATCH = 2
SEQ = 8192
DEPTH = 2

CHUNK = 64
HEAD_DIM = 64
A_HEADS = 8
A_WIDTH = A_HEADS * HEAD_DIM
A_LOOKBACK = 8
A_BAND = (A_LOOKBACK + 1) * CHUNK
A_MAX_REL = 256
A_REL_SIZE = A_MAX_REL + CHUNK
LRU_WIDTH = 512
LRU_BLOCKS = 8
LRU_BLOCK_DIM = LRU_WIDTH // LRU_BLOCKS
CONV_WIDTH = 4
LRU_C = 8.0
C_HEADS = 4
C_QK_WIDTH = C_HEADS * 2 * HEAD_DIM
C_V_WIDTH = C_HEADS * 2 * HEAD_DIM
Q_BLOCK = 128
ROPE_THETA = 10000.0
N_BRANCHES = 3
BRANCH_WIDTH = 512
IN_COLS = 3 * A_WIDTH + 2 * LRU_WIDTH + 2 * C_QK_WIDTH + C_V_WIDTH + N_BRANCHES * D_MODEL
PEER_HEADS = 8
PEER_N_KEYS = 128
PEER_N_EXPERTS = PEER_N_KEYS * PEER_N_KEYS
PEER_QUERY_DIM = 256
PEER_HALF = PEER_QUERY_DIM // 2
PEER_TOPK = 16
PEER_BLOCK = 128
DEEPNORM_ALPHA = (2 * DEPTH) ** 0.25
DEEPNORM_BETA = (8 * DEPTH) ** -0.25
LN_EPS = 1e-5
NEG_INF = -1e30

kernel_name = "hybrid_chunk_band_rglru_diffattn_peer"


def layer_norm(x, g, b):
    xf = x.astype(jnp.float32)
    mu = xf.mean(-1, keepdims=True)
    var = jnp.square(xf - mu).mean(-1, keepdims=True)
    return ((xf - mu) * lax.rsqrt(var + LN_EPS) * g.astype(jnp.float32) + b.astype(jnp.float32)).astype(x.dtype)


def rms_norm(x, g):
    xf = x.astype(jnp.float32)
    ms = jnp.square(xf).mean(-1, keepdims=True)
    return (xf * lax.rsqrt(ms + LN_EPS) * g.astype(jnp.float32)).astype(x.dtype)


def rope_tables(positions):
    inv_freq = jnp.power(ROPE_THETA, -jnp.arange(0, HEAD_DIM, 2, dtype=jnp.float32) / HEAD_DIM)
    ang = positions.astype(jnp.float32)[..., None] * inv_freq
    return jnp.cos(ang), jnp.sin(ang)


def apply_rope(t, cos, sin):
    extra = (1,) * (t.ndim - 3)
    c = cos.reshape(cos.shape[:2] + extra + cos.shape[2:]).astype(t.dtype)
    s = sin.reshape(sin.shape[:2] + extra + sin.shape[2:]).astype(t.dtype)
    half = HEAD_DIM // 2
    t1, t2 = t[..., :half], t[..., half:]
    return jnp.concatenate([t1 * c - t2 * s, t2 * c + t1 * s], axis=-1)


def chunk_band_attention(q, k, v, rel_bias):
    B, S, H, d = q.shape
    nc = S // CHUNK
    qc = q.reshape(B, nc, CHUNK, H, d)

    def band(t):
        tc = t.reshape(B, nc, CHUNK, H, d)
        tp = jnp.pad(tc, ((0, 0), (A_LOOKBACK, 0), (0, 0), (0, 0), (0, 0)))
        return jnp.concatenate([tp[:, j:j + nc] for j in range(A_LOOKBACK + 1)], axis=2)

    kb, vb = band(k), band(v)
    s = jnp.einsum('bcqhd,bckhd->bchqk', qc, kb).astype(jnp.float32) * (d ** -0.5)
    qi = jnp.arange(CHUNK)[:, None]
    kj = jnp.arange(A_BAND)[None, :]
    dist = qi + A_LOOKBACK * CHUNK - kj
    idx = jnp.clip(dist, -(CHUNK - 1), A_MAX_REL) + (CHUNK - 1)
    bias = rel_bias.astype(jnp.float32)[:, idx]
    valid = (jnp.arange(nc)[:, None] - A_LOOKBACK + kj // CHUNK) >= 0
    s = jnp.where(valid[None, :, None, None, :], s + bias[None, None], NEG_INF)
    p = jax.nn.softmax(s, axis=-1).astype(v.dtype)
    o = jnp.einsum('bchqk,bckhd->bcqhd', p, vb)
    return o.reshape(B, S, H * d)


def _linear_combine(left, right):
    a_l, b_l = left
    a_r, b_r = right
    return a_l * a_r, a_r * b_l + b_r


def rglru_block(bx, bg, conv_w, conv_b, w_r, b_r, w_i, b_i, lru_lambda):
    B, S, W = bx.shape
    xp = jnp.pad(bx, ((0, 0), (CONV_WIDTH - 1, 0), (0, 0)))
    xc = conv_b + xp[:, 0:S] * conv_w[0]
    for tap in range(1, CONV_WIDTH):
        xc = xc + xp[:, tap:tap + S] * conv_w[tap]
    xh = xc.reshape(B, S, LRU_BLOCKS, LRU_BLOCK_DIM)
    r = jax.nn.sigmoid(jnp.einsum('bsgi,gij->bsgj', xh, w_r) + b_r).reshape(B, S, W)
    i = jax.nn.sigmoid(jnp.einsum('bsgi,gij->bsgj', xh, w_i) + b_i).reshape(B, S, W)
    log_a = LRU_C * r.astype(jnp.float32) * jax.nn.log_sigmoid(lru_lambda.astype(jnp.float32))
    a = jnp.exp(log_a)
    u = jnp.sqrt(-jnp.expm1(2.0 * log_a)) * (i * xc).astype(jnp.float32)
    _, h = lax.associative_scan(_linear_combine, (a, u), axis=1)
    return h.astype(bx.dtype) * jax.nn.gelu(bg)


def diff_attention(q, k, v, lam, lam_init, subln_g):
    B, S, H, _, d = q.shape
    nb = S // Q_BLOCK
    qb = jnp.moveaxis(q.reshape(B, nb, Q_BLOCK, H, 2, d), 1, 0)
    key_chunk = jnp.arange(S) // CHUNK
    scale = d ** -0.5

    def one_block(args):
        q_blk, bi = args
        q_chunk = (bi * Q_BLOCK + jnp.arange(Q_BLOCK)) // CHUNK
        mask = key_chunk[None, :] <= q_chunk[:, None]
        s = jnp.einsum('bqhmd,bkhmd->bhmqk', q_blk, k).astype(jnp.float32) * scale
        s = jnp.where(mask, s, NEG_INF)
        p = jax.nn.softmax(s, axis=-1)
        att = (p[:, :, 0] - lam * p[:, :, 1]).astype(v.dtype)
        return jnp.einsum('bhqk,bkhe->bqhe', att, v)

    o = lax.map(one_block, (qb, jnp.arange(nb)))
    o = jnp.moveaxis(o, 0, 1).reshape(B, S, H, 2 * d)
    o = rms_norm(o, subln_g) * (1.0 - lam_init)
    return o.reshape(B, S, H * 2 * d)


def token_mixers(h, cos, sin, lam_init, w_in, gate_bias, rel_bias, conv_w, conv_b, w_r, b_r, w_i, b_i,
                 lru_lambda, diff_lambda, subln_g, w_branch, w_out):
    B, S, _ = h.shape
    z = h @ w_in
    sizes = (A_WIDTH, A_WIDTH, A_WIDTH, LRU_WIDTH, LRU_WIDTH, C_QK_WIDTH, C_QK_WIDTH, C_V_WIDTH,
             N_BRANCHES * D_MODEL)
    points = [int(p) for p in np.cumsum(sizes)[:-1]]
    aq, ak, av, bx, bg, cq, ck, cv, gl = jnp.split(z, points, axis=-1)
    ya = chunk_band_attention(aq.reshape(B, S, A_HEADS, HEAD_DIM), ak.reshape(B, S, A_HEADS, HEAD_DIM),
                              av.reshape(B, S, A_HEADS, HEAD_DIM), rel_bias)
    yb = rglru_block(bx, bg, conv_w, conv_b, w_r, b_r, w_i, b_i, lru_lambda)
    cq = apply_rope(cq.reshape(B, S, C_HEADS, 2, HEAD_DIM), cos, sin)
    ck = apply_rope(ck.reshape(B, S, C_HEADS, 2, HEAD_DIM), cos, sin)
    dl = diff_lambda.astype(jnp.float32)
    lam = jnp.exp(jnp.sum(dl[0] * dl[1])) - jnp.exp(jnp.sum(dl[2] * dl[3])) + lam_init
    yc = diff_attention(cq, ck, cv.reshape(B, S, C_HEADS, 2 * HEAD_DIM), lam, lam_init, subln_g)
    branches = jnp.stack([ya, yb, yc], axis=2)
    proj = jnp.einsum('bsnw,nwd->bsnd', branches, w_branch)
    gates = jax.nn.sigmoid(gl.reshape(B, S, N_BRANCHES, D_MODEL) + gate_bias)
    merged = jnp.sum(gates * proj, axis=2)
    return merged @ w_out


def peer_ffn(h, w_q, sub_keys, u, v):
    B, S, D = h.shape
    xt = h.reshape((B * S) // PEER_BLOCK, PEER_BLOCK, D)

    def one_block(xb):
        q = (xb @ w_q).reshape(PEER_BLOCK, PEER_HEADS, 2, PEER_HALF)
        s = jnp.einsum('thpd,hpnd->thpn', q, sub_keys).astype(jnp.float32)
        sv, si = lax.top_k(s, PEER_TOPK)
        cand = (sv[:, :, 0, :, None] + sv[:, :, 1, None, :]).reshape(PEER_BLOCK, PEER_HEADS, PEER_TOPK * PEER_TOPK)
        cidx = (si[:, :, 0, :, None] * PEER_N_KEYS + si[:, :, 1, None, :]).reshape(PEER_BLOCK, PEER_HEADS, PEER_TOPK * PEER_TOPK)
        fs, fi = lax.top_k(cand, PEER_TOPK)
        experts = jnp.take_along_axis(cidx, fi, axis=-1)
        g = jax.nn.softmax(fs, axis=-1).astype(xb.dtype)
        hid = jax.nn.gelu(jnp.einsum('td,thkd->thk', xb, u[experts]))
        return jnp.einsum('thk,thkd->td', g * hid, v[experts])

    return lax.map(one_block, xt).reshape(B, S, D)


def setup_inputs(seed: int = 0) -> dict:
    key = jax.random.key(seed)
    ks = jax.random.split(key, 24)
    f32 = jnp.float32
    L = DEPTH

    def nrm(k, shape, scale):
        return jax.random.normal(k, shape, f32) * scale

    x = nrm(ks[0], (BATCH, SEQ, D_MODEL), 1.0)
    offset = jax.random.randint(ks[1], (BATCH, 1), 0, 1 << 16, dtype=jnp.int32)
    positions = offset + jnp.arange(SEQ, dtype=jnp.int32)[None, :]
    w_in = nrm(ks[2], (L, D_MODEL, IN_COLS), D_MODEL ** -0.5)
    gate_bias = nrm(ks[3], (L, N_BRANCHES, D_MODEL), 0.1)
    rel_bias = nrm(ks[4], (L, A_HEADS, A_REL_SIZE), 0.3)
    conv_w = nrm(ks[5], (L, CONV_WIDTH, LRU_WIDTH), CONV_WIDTH ** -0.5)
    conv_b = nrm(ks[6], (L, LRU_WIDTH), 0.01)
    lru_w_r = nrm(ks[7], (L, LRU_BLOCKS, LRU_BLOCK_DIM, LRU_BLOCK_DIM), LRU_BLOCK_DIM ** -0.5)
    lru_b_r = nrm(ks[8], (L, LRU_BLOCKS, LRU_BLOCK_DIM), 0.01)
    lru_w_i = nrm(ks[9], (L, LRU_BLOCKS, LRU_BLOCK_DIM, LRU_BLOCK_DIM), LRU_BLOCK_DIM ** -0.5)
    lru_b_i = nrm(ks[10], (L, LRU_BLOCKS, LRU_BLOCK_DIM), 0.01)
    a_pow = jax.random.uniform(ks[11], (L, LRU_WIDTH), f32, 0.9, 0.999)
    a_base = a_pow ** (1.0 / LRU_C)
    lru_lambda = jnp.log(a_base) - jnp.log1p(-a_base)
    diff_lambda = nrm(ks[12], (L, 4, HEAD_DIM), 0.1)
    diff_subln_g = 1.0 + nrm(ks[13], (L, 2 * HEAD_DIM), 0.02)
    w_branch = nrm(ks[14], (L, N_BRANCHES, BRANCH_WIDTH, D_MODEL), DEEPNORM_BETA * BRANCH_WIDTH ** -0.5)
    w_out = nrm(ks[15], (L, D_MODEL, D_MODEL), DEEPNORM_BETA * D_MODEL ** -0.5)
    ln1_g = 1.0 + nrm(ks[16], (L, D_MODEL), 0.02)
    ln1_b = nrm(ks[17], (L, D_MODEL), 0.02)
    peer_w_q = nrm(ks[18], (L, D_MODEL, PEER_HEADS * PEER_QUERY_DIM), D_MODEL ** -0.5)
    peer_sub_keys = nrm(ks[19], (L, PEER_HEADS, 2, PEER_N_KEYS, PEER_HALF), PEER_HALF ** -0.5)
    peer_u = nrm(ks[20], (L, PEER_N_EXPERTS, D_MODEL), D_MODEL ** -0.5)
    peer_v = nrm(ks[21], (L, PEER_N_EXPERTS, D_MODEL), DEEPNORM_BETA)
    ln2_g = 1.0 + nrm(ks[22], (L, D_MODEL), 0.02)
    ln2_b = nrm(ks[23], (L, D_MODEL), 0.02)
    return {"x": x, "positions": positions, "w_in": w_in, "gate_bias": gate_bias, "rel_bias": rel_bias,
            "conv_w": conv_w, "conv_b": conv_b, "lru_w_r": lru_w_r, "lru_b_r": lru_b_r,
            "lru_w_i": lru_w_i, "lru_b_i": lru_b_i, "lru_lambda": lru_lambda, "diff_lambda": diff_lambda,
            "diff_subln_g": diff_subln_g, "w_branch": w_branch, "w_out": w_out, "ln1_g": ln1_g,
            "ln1_b": ln1_b, "peer_w_q": peer_w_q, "peer_sub_keys": peer_sub_keys, "peer_u": peer_u,
            "peer_v": peer_v, "ln2_g": ln2_g, "ln2_b": ln2_b}


def reference(x, positions, w_in, gate_bias, rel_bias, conv_w, conv_b, lru_w_r, lru_b_r, lru_w_i, lru_b_i,
              lru_lambda, diff_lambda, diff_subln_g, w_branch, w_out, ln1_g, ln1_b, peer_w_q, peer_sub_keys,
              peer_u, peer_v, ln2_g, ln2_b):
    cos, sin = rope_tables(positions)
    h = x
    for l in range(DEPTH):
        lam_init = 0.8 - 0.6 * math.exp(-0.3 * l)
        mix = token_mixers(h, cos, sin, lam_init, w_in[l], gate_bias[l], rel_bias[l], conv_w[l], conv_b[l],
                           lru_w_r[l], lru_b_r[l], lru_w_i[l], lru_b_i[l], lru_lambda[l], diff_lambda[l],
                           diff_subln_g[l], w_branch[l], w_out[l])
        h = layer_norm(DEEPNORM_ALPHA * h + mix, ln1_g[l], ln1_b[l])
        ff = peer_ffn(h, peer_w_q[l], peer_sub_keys[l], peer_u[l], peer_v[l])
        h = layer_norm(DEEPNORM_ALPHA * h + ff, ln2_g[l], ln2_b[l])
    return h
```

```python
import functools
import math

import numpy as np
import jax
import jax.numpy as jnp
from jax import lax
from jax.experimental import pallas as pl
from jax.experimental.pallas import tpu as pltpu

D_MODEL = 1024
DEPTH = 2
CHUNK = 64
HEAD_DIM = 64
A_HEADS = 8
A_LOOKBACK = 8
A_MAX_REL = 256
LRU_WIDTH = 512
LRU_BLOCKS = 8
LRU_BLOCK_DIM = 64
CONV_WIDTH = 4
LRU_C = 8.0
C_HEADS = 4
ROPE_THETA = 10000.0
N_BRANCHES = 3
BRANCH_WIDTH = 512
PEER_HEADS = 8
PEER_N_KEYS = 128
PEER_N_EXPERTS = PEER_N_KEYS * PEER_N_KEYS
PEER_HALF = 128
PEER_TOPK = 16
DEEPNORM_ALPHA = (2 * DEPTH) ** 0.25
LN_EPS = 1e-5
NEG_INF = -1e30

LANES = 128
VMEM_LIMIT = 56 * 1024 * 1024

COL_AQ, COL_AK, COL_AV = 0, 512, 1024
COL_BX, COL_BG = 1536, 2048
COL_CQ, COL_CK, COL_CV = 2560, 3072, 3584
COL_GL = 4096
IN_COLS = 7168

F32 = jnp.float32
BF16 = jnp.bfloat16
_NT = (((1,), (1,)), ((), ()))


def _params(*sem):
    return pltpu.CompilerParams(dimension_semantics=sem, vmem_limit_bytes=VMEM_LIMIT)


def _layer_norm(y, g, b):
    mu = jnp.mean(y, axis=-1, keepdims=True)
    d = y - mu
    var = jnp.mean(d * d, axis=-1, keepdims=True)
    return d * lax.rsqrt(var + LN_EPS) * g + b


def _mm_kernel(x_ref, w_ref, o_ref):
    o_ref[...] = jnp.dot(x_ref[...], w_ref[...], preferred_element_type=F32).astype(o_ref.dtype)


def _matmul(x, w, out_dtype, tm=512, tn=1024):
    m, k = x.shape
    n = w.shape[1]
    return pl.pallas_call(
        _mm_kernel,
        grid=(n // tn, m // tm),
        in_specs=[pl.BlockSpec((tm, k), lambda j, i: (i, 0)),
                  pl.BlockSpec((k, tn), lambda j, i: (0, j))],
        out_specs=pl.BlockSpec((tm, tn), lambda j, i: (i, j)),
        out_shape=jax.ShapeDtypeStruct((m, n), out_dtype),
        compiler_params=_params("parallel", "parallel"),
        name="in_proj",
    )(x, w)


def _rope_kernel(pos_ref, inv_ref, q_ref, k_ref, qo_ref, ko_ref):
    tt = pos_ref.shape[0]
    ang = pos_ref[...].astype(F32) * inv_ref[...]
    c = jnp.cos(ang)
    s = jnp.sin(ang)
    lane = lax.broadcasted_iota(jnp.int32, (tt, LANES), 1)
    first = (lane % HEAD_DIM) < (HEAD_DIM // 2)
    s_signed = jnp.where(first, -s, s)

    def rot(x):
        partner = jnp.where(first, pltpu.roll(x, LANES - HEAD_DIM // 2, 1), pltpu.roll(x, HEAD_DIM // 2, 1))
        return x * c + partner * s_signed

    for blk in range(q_ref.shape[1] // LANES):
        sl = slice(blk * LANES, (blk + 1) * LANES)
        qo_ref[:, sl] = (rot(q_ref[:, sl].astype(F32)) * (HEAD_DIM ** -0.5)).astype(qo_ref.dtype)
        ko_ref[:, sl] = rot(k_ref[:, sl].astype(F32)).astype(ko_ref.dtype)


def _rope(z, pos_col, inv_tiled, tt=512):
    t = z.shape[0]
    w = 512
    return pl.pallas_call(
        _rope_kernel,
        grid=(t // tt,),
        in_specs=[pl.BlockSpec((tt, 1), lambda i: (i, 0)),
                  pl.BlockSpec((1, LANES), lambda i: (0, 0)),
                  pl.BlockSpec((tt, w), lambda i: (i, COL_CQ // w)),
                  pl.BlockSpec((tt, w), lambda i: (i, COL_CK // w))],
        out_specs=[pl.BlockSpec((tt, w), lambda i: (i, 0)),
                   pl.BlockSpec((tt, w), lambda i: (i, 0))],
        out_shape=[jax.ShapeDtypeStruct((t, w), BF16), jax.ShapeDtypeStruct((t, w), BF16)],
        compiler_params=_params("parallel"),
        name="rope",
    )(pos_col, inv_tiled, z, z)


BAND_Q = 2 * CHUNK
BAND_K = (A_LOOKBACK + 2) * CHUNK
BAND_PAD = A_LOOKBACK * CHUNK


def _band_bias_table(rel_bias):
    qi = np.arange(BAND_Q)[:, None]
    kj = np.arange(BAND_K)[None, :]
    dist = qi + BAND_PAD - kj
    idx = np.clip(dist, -(CHUNK - 1), A_MAX_REL) + (CHUNK - 1)
    q_chunk = qi // CHUNK + A_LOOKBACK
    k_chunk = kj // CHUNK
    allowed = (k_chunk <= q_chunk) & (k_chunk >= q_chunk - A_LOOKBACK)
    bias = rel_bias.astype(F32)[:, idx]
    return jnp.where(jnp.asarray(allowed)[None], bias, NEG_INF)


def _band_kernel(q_ref, k_ref, v_ref, tab_ref, o_ref, kp_ref, vp_ref):
    c = pl.program_id(2)

    @pl.when(c == 0)
    def _():
        zeros = jnp.zeros((BAND_PAD, LANES), kp_ref.dtype)
        kp_ref[0:BAND_PAD, :] = zeros
        vp_ref[0:BAND_PAD, :] = zeros
        kp_ref[BAND_PAD:, :] = k_ref[...]
        vp_ref[BAND_PAD:, :] = v_ref[...]

    start = pl.multiple_of(c * BAND_Q, BAND_Q)
    kw = kp_ref[pl.ds(start, BAND_K), :]
    vw = vp_ref[pl.ds(start, BAND_K), :]
    q = q_ref[...]
    lane = lax.broadcasted_iota(jnp.int32, (BAND_Q, LANES), 1)
    col = lax.broadcasted_iota(jnp.int32, (BAND_Q, BAND_K), 1)
    valid = (col + c * BAND_Q) >= BAND_PAD
    outs = []
    for hh in range(2):
        keep = (lane < HEAD_DIM) if hh == 0 else (lane >= HEAD_DIM)
        qm = jnp.where(keep, q, jnp.zeros_like(q))
        s = lax.dot_general(qm, kw, _NT, preferred_element_type=F32) * (HEAD_DIM ** -0.5)
        s = jnp.where(valid, s + tab_ref[hh], NEG_INF)
        m = jnp.max(s, axis=-1, keepdims=True)
        e = jnp.exp(s - m)
        p = e / jnp.sum(e, axis=-1, keepdims=True)
        outs.append(jnp.dot(p.astype(vw.dtype), vw, preferred_element_type=F32))
    o_ref[...] = jnp.where(lane < HEAD_DIM, outs[0], outs[1]).astype(o_ref.dtype)


def _band_attention(z, table, batch, seq):
    t = z.shape[0]
    nq = seq // BAND_Q
    return pl.pallas_call(
        _band_kernel,
        grid=(batch, A_HEADS // 2, nq),
        in_specs=[pl.BlockSpec((BAND_Q, LANES), lambda b, g, c: (b * nq + c, COL_AQ // LANES + g)),
                  pl.BlockSpec((seq, LANES), lambda b, g, c: (b, COL_AK // LANES + g)),
                  pl.BlockSpec((seq, LANES), lambda b, g, c: (b, COL_AV // LANES + g)),
                  pl.BlockSpec((2, BAND_Q, BAND_K), lambda b, g, c: (g, 0, 0))],
        out_specs=pl.BlockSpec((BAND_Q, LANES), lambda b, g, c: (b * nq + c, g)),
        out_shape=jax.ShapeDtypeStruct((t, A_HEADS * HEAD_DIM), BF16),
        scratch_shapes=[pltpu.VMEM((seq + BAND_PAD, LANES), z.dtype),
                        pltpu.VMEM((seq + BAND_PAD, LANES), z.dtype)],
        compiler_params=_params("parallel", "parallel", "arbitrary"),
        name="band_attn",
    )(z, z, z, table)


LRU_ROWS = 8


def _lru_kernel(bx_ref, bg_ref, cw_ref, cb_ref, wr_ref, br_ref, wi_ref, bi_ref, lam_ref, o_ref,
                tail_ref, h_ref, a_ref, u_ref):
    tt = bx_ref.shape[0]

    @pl.when(pl.program_id(1) == 0)
    def _():
        tail_ref[...] = jnp.zeros_like(tail_ref)
        h_ref[...] = jnp.zeros_like(h_ref)

    x = bx_ref[...].astype(F32)
    xx = jnp.concatenate([tail_ref[...], x], axis=0)
    tail_ref[...] = x[tt - 8:, :]
    xc = cb_ref[...] + xx[5:5 + tt, :] * cw_ref[0:1, :]
    for tap in range(1, CONV_WIDTH):
        xc = xc + xx[5 + tap:5 + tap + tt, :] * cw_ref[tap:tap + 1, :]
    xcb = xc.astype(BF16)
    r = jax.nn.sigmoid(jnp.dot(xcb, wr_ref[...], preferred_element_type=F32) + br_ref[...])
    i = jax.nn.sigmoid(jnp.dot(xcb, wi_ref[...], preferred_element_type=F32) + bi_ref[...])
    log_a = LRU_C * r * jax.nn.log_sigmoid(lam_ref[...])
    a = jnp.exp(log_a)
    a_ref[...] = a
    u_ref[...] = jnp.sqrt(1.0 - a * a) * (i * xc)

    def group(gi, h):
        base = pl.multiple_of(gi * LRU_ROWS, LRU_ROWS)
        for rr in range(LRU_ROWS):
            h = a_ref[pl.ds(base + rr, 1), :] * h + u_ref[pl.ds(base + rr, 1), :]
            u_ref[pl.ds(base + rr, 1), :] = h
        return h

    h_ref[...] = lax.fori_loop(0, tt // LRU_ROWS, group, h_ref[...])
    o_ref[...] = (u_ref[...] * jax.nn.gelu(bg_ref[...].astype(F32))).astype(o_ref.dtype)


def _block_diag(w):
    eye = jnp.eye(LRU_BLOCKS, dtype=w.dtype)
    return jnp.einsum('gij,gh->gihj', w, eye).reshape(LRU_WIDTH, LRU_WIDTH)


def _rglru(z, conv_w, conv_b, w_r, b_r, w_i, b_i, lru_lambda, batch, seq, tt=512):
    t = z.shape[0]
    nt = seq // tt
    w = LRU_WIDTH
    row = lambda a: a.reshape(1, w).astype(F32)
    full = lambda shape: pl.BlockSpec(shape, lambda b, j: (0,) * len(shape))
    return pl.pallas_call(
        _lru_kernel,
        grid=(batch, nt),
        in_specs=[pl.BlockSpec((tt, w), lambda b, j: (b * nt + j, COL_BX // w)),
                  pl.BlockSpec((tt, w), lambda b, j: (b * nt + j, COL_BG // w)),
                  full((CONV_WIDTH, w)), full((1, w)), full((w, w)), full((1, w)), full((w, w)), full((1, w)),
                  full((1, w))],
        out_specs=pl.BlockSpec((tt, w), lambda b, j: (b * nt + j, 0)),
        out_shape=jax.ShapeDtypeStruct((t, w), BF16),
        scratch_shapes=[pltpu.VMEM((8, w), F32), pltpu.VMEM((1, w), F32),
                        pltpu.VMEM((tt, w), F32), pltpu.VMEM((tt, w), F32)],
        compiler_params=_params("parallel", "arbitrary"),
        name="rglru",
    )(z, z, conv_w.astype(F32), row(conv_b), _block_diag(w_r).astype(BF16), row(b_r),
      _block_diag(w_i).astype(BF16), row(b_i), row(lru_lambda))


DIFF_TQ = 256
DIFF_TK = 256


def _diff_kernel(q_ref, k_ref, v_ref, dl_ref, g_ref, o_ref, *, lam_init):
    qi = pl.program_id(2)
    tq, tk = DIFF_TQ, DIFF_TK
    q = q_ref[...]
    lane = lax.broadcasted_iota(jnp.int32, (tq, LANES), 1)
    q1 = jnp.where(lane < HEAD_DIM, q, jnp.zeros_like(q))
    q2 = jnp.where(lane >= HEAD_DIM, q, jnp.zeros_like(q))

    def update(s, vb, m, l, acc):
        m_new = jnp.maximum(m, jnp.max(s, axis=-1, keepdims=True))
        alpha = jnp.exp(m - m_new)
        p = jnp.exp(s - m_new)
        l_new = alpha * l + jnp.sum(p, axis=-1, keepdims=True)
        acc_new = alpha * acc + jnp.dot(p.astype(vb.dtype), vb, preferred_element_type=F32)
        return m_new, l_new, acc_new

    def step(j, carry, masked):
        m1, l1, acc1, m2, l2, acc2 = carry
        start = pl.multiple_of(j * tk, tk)
        kb = k_ref[pl.ds(start, tk), :]
        vb = v_ref[pl.ds(start, tk), :]
        s1 = lax.dot_general(q1, kb, _NT, preferred_element_type=F32)
        s2 = lax.dot_general(q2, kb, _NT, preferred_element_type=F32)
        if masked:
            row = lax.broadcasted_iota(jnp.int32, (tq, tk), 0)
            col = lax.broadcasted_iota(jnp.int32, (tq, tk), 1)
            ok = (col // CHUNK) <= (row // CHUNK)
            s1 = jnp.where(ok, s1, NEG_INF)
            s2 = jnp.where(ok, s2, NEG_INF)
        m1, l1, acc1 = update(s1, vb, m1, l1, acc1)
        m2, l2, acc2 = update(s2, vb, m2, l2, acc2)
        return m1, l1, acc1, m2, l2, acc2

    stat = lambda v: jnp.full((tq, 1), v, F32)
    zacc = jnp.zeros((tq, LANES), F32)
    carry = (stat(NEG_INF), stat(0.0), zacc, stat(NEG_INF), stat(0.0), zacc)
    carry = lax.fori_loop(0, qi, lambda j, cr: step(j, cr, False), carry)
    m1, l1, acc1, m2, l2, acc2 = step(qi, carry, True)

    dl = dl_ref[...]
    lam = (jnp.exp(jnp.sum(dl[0:1] * dl[1:2], axis=-1, keepdims=True))
           - jnp.exp(jnp.sum(dl[2:3] * dl[3:4], axis=-1, keepdims=True)) + lam_init)
    o = acc1 / l1 - lam * (acc2 / l2)
    ms = jnp.mean(o * o, axis=-1, keepdims=True)
    o = o * lax.rsqrt(ms + LN_EPS) * g_ref[...] * (1.0 - lam_init)
    o_ref[...] = o.astype(o_ref.dtype)


def _diff_attention(qr, kr, z, diff_lambda, subln_g, lam_init, batch, seq):
    t = z.shape[0]
    nq = seq // DIFF_TQ
    return pl.pallas_call(
        functools.partial(_diff_kernel, lam_init=lam_init),
        grid=(batch, C_HEADS, nq),
        in_specs=[pl.BlockSpec((DIFF_TQ, LANES), lambda b, h, i: (b * nq + i, h)),
                  pl.BlockSpec((seq, LANES), lambda b, h, i: (b, h)),
                  pl.BlockSpec((seq, LANES), lambda b, h, i: (b, COL_CV // LANES + h)),
                  pl.BlockSpec((4, HEAD_DIM), lambda b, h, i: (0, 0)),
                  pl.BlockSpec((1, LANES), lambda b, h, i: (0, 0))],
        out_specs=pl.BlockSpec((DIFF_TQ, LANES), lambda b, h, i: (b * nq + i, h)),
        out_shape=jax.ShapeDtypeStruct((t, C_HEADS * 2 * HEAD_DIM), BF16),
        compiler_params=_params("parallel", "parallel", "arbitrary"),
        name="diff_attn",
    )(qr, kr, z, diff_lambda.astype(F32), subln_g.reshape(1, LANES).astype(F32))


def _merge_kernel(ya_ref, yb_ref, yc_ref, gl0_ref, gl1_ref, gl2_ref, gb_ref, wb_ref, wo_ref, h_ref, g_ref, b_ref,
                  o_ref, ob_ref):
    merged = None
    for n, (y_ref, gl_ref) in enumerate(((ya_ref, gl0_ref), (yb_ref, gl1_ref), (yc_ref, gl2_ref))):
        proj = jnp.dot(y_ref[...], wb_ref[n], preferred_element_type=F32)
        gate = jax.nn.sigmoid(gl_ref[...].astype(F32) + gb_ref[n:n + 1, :])
        merged = gate * proj if merged is None else merged + gate * proj
    mix = jnp.dot(merged.astype(BF16), wo_ref[...], preferred_element_type=F32)
    y = _layer_norm(DEEPNORM_ALPHA * h_ref[...] + mix, g_ref[...], b_ref[...])
    o_ref[...] = y
    ob_ref[...] = y.astype(ob_ref.dtype)


def _merge(ya, yb, yc, z, gate_bias, w_branch, w_out, h, ln_g, ln_b, tt=256):
    t = h.shape[0]
    d = D_MODEL
    bw = BRANCH_WIDTH
    ytile = pl.BlockSpec((tt, bw), lambda i: (i, 0))
    gl = lambda n: pl.BlockSpec((tt, d), lambda i: (i, COL_GL // d + n))
    full = lambda shape: pl.BlockSpec(shape, lambda i: (0,) * len(shape))
    return pl.pallas_call(
        _merge_kernel,
        grid=(t // tt,),
        in_specs=[ytile, ytile, ytile, gl(0), gl(1), gl(2), full((N_BRANCHES, d)), full((N_BRANCHES, bw, d)),
                  full((d, d)), pl.BlockSpec((tt, d), lambda i: (i, 0)), full((1, d)), full((1, d))],
        out_specs=[pl.BlockSpec((tt, d), lambda i: (i, 0)), pl.BlockSpec((tt, d), lambda i: (i, 0))],
        out_shape=[jax.ShapeDtypeStruct((t, d), F32), jax.ShapeDtypeStruct((t, d), BF16)],
        compiler_params=_params("parallel"),
        name="merge_out_ln",
    )(ya, yb, yc, z, z, z, gate_bias.astype(F32), w_branch.astype(BF16), w_out.astype(BF16), h,
      ln_g.reshape(1, d).astype(F32), ln_b.reshape(1, d).astype(F32))


def _top_rows(s, k):
    rows = []
    cur = s
    for _ in range(k):
        m = jnp.max(cur, axis=0, keepdims=True)
        rows.append(m)
        cur = jnp.where(cur == m, -jnp.inf, cur)
    return jnp.concatenate(rows, axis=0)


def _route_kernel(h_ref, wq_ref, keys_ref, s1_ref, s2_ref, e1_ref, e2_ref, tau_ref):
    tt = h_ref.shape[0]
    qt = lax.dot_general(wq_ref[...], h_ref[...], _NT, preferred_element_type=F32).astype(BF16)
    for h in range(PEER_HEADS):
        sc = []
        for p in range(2):
            r0 = (h * 2 + p) * PEER_HALF
            sc.append(jnp.dot(keys_ref[h, p], qt[r0:r0 + PEER_HALF, :], preferred_element_type=F32))
        s1, s2 = sc
        a = _top_rows(s1, PEER_TOPK)
        b = _top_rows(s2, PEER_TOPK)
        cand = jnp.concatenate([a[k:k + 1, :] + b for k in range(PEER_TOPK)], axis=0)
        tau = _top_rows(cand, PEER_TOPK)[PEER_TOPK - 1:PEER_TOPK, :]
        a1 = a[0:1, :]
        b1 = b[0:1, :]
        zsum = jnp.sum(jnp.where(cand >= tau, jnp.exp(cand - (a1 + b1)), 0.0), axis=0, keepdims=True)
        s1_ref[h] = s1
        s2_ref[h] = s2
        e1_ref[h] = jnp.exp(s1 - a1) / zsum
        e2_ref[h] = jnp.exp(s2 - b1)
        tau_ref[h] = jnp.broadcast_to(tau, (8, tt))


def _peer_route(hb, wq_t, keys, tt=256):
    t = hb.shape[0]
    big = jax.ShapeDtypeStruct((PEER_HEADS, PEER_N_KEYS, t), F32)
    bspec = pl.BlockSpec((PEER_HEADS, PEER_N_KEYS, tt), lambda i: (0, 0, i))
    return pl.pallas_call(
        _route_kernel,
        grid=(t // tt,),
        in_specs=[pl.BlockSpec((tt, D_MODEL), lambda i: (i, 0)),
                  pl.BlockSpec(wq_t.shape, lambda i: (0, 0)),
                  pl.BlockSpec(keys.shape, lambda i: (0, 0, 0, 0))],
        out_specs=[bspec, bspec, bspec, bspec, pl.BlockSpec((PEER_HEADS, 8, tt), lambda i: (0, 0, i))],
        out_shape=[big, big, big, big, jax.ShapeDtypeStruct((PEER_HEADS, 8, t), F32)],
        compiler_params=_params("parallel"),
        name="peer_route",
    )(hb, wq_t, keys)


PEER_TT = 512
PEER_IG = 8
PEER_E = PEER_IG * PEER_N_KEYS


def _peer_kernel(hb_ref, u_ref, vt_ref, s1_ref, s2_ref, e1_ref, e2_ref, tau_ref, h_ref, g_ref, b_ref,
                 o_ref, ob_ref, acc_ref, p_ref):
    e = pl.program_id(1)
    tt = hb_ref.shape[0]

    @pl.when(e == 0)
    def _():
        acc_ref[...] = jnp.zeros_like(acc_ref)

    act = jax.nn.gelu(lax.dot_general(u_ref[...], hb_ref[...], _NT, preferred_element_type=F32))
    for lc in range(tt // LANES):
        ls = slice(lc * LANES, (lc + 1) * LANES)
        for ii in range(PEER_IG):
            w = jnp.zeros((PEER_N_KEYS, LANES), F32)
            for h in range(PEER_HEADS):
                x = s1_ref[h, 0, ii:ii + 1, ls] + s2_ref[h, :, ls]
                val = e1_ref[h, 0, ii:ii + 1, ls] * e2_ref[h, :, ls]
                w = w + jnp.where(x >= tau_ref[h, 0:1, ls], val, 0.0)
            rs = slice(ii * PEER_N_KEYS, (ii + 1) * PEER_N_KEYS)
            p_ref[rs, ls] = (w * act[rs, ls]).astype(p_ref.dtype)
    acc_ref[...] += jnp.dot(vt_ref[...], p_ref[...], preferred_element_type=F32)

    @pl.when(e == pl.num_programs(1) - 1)
    def _():
        ff = acc_ref[...].T
        y = _layer_norm(DEEPNORM_ALPHA * h_ref[...] + ff, g_ref[...], b_ref[...])
        o_ref[...] = y
        ob_ref[...] = y.astype(ob_ref.dtype)


def _peer_dense(hb, h, u_b, vt_b, s1, s2, e1, e2, tau, ln_g, ln_b):
    t = h.shape[0]
    d = D_MODEL
    tt = min(PEER_TT, t)
    ne = PEER_N_EXPERTS // PEER_E
    big = pl.BlockSpec((PEER_HEADS, PEER_N_KEYS, tt), lambda i, e: (0, 0, i))
    grp = pl.BlockSpec((PEER_HEADS, 1, PEER_IG, tt), lambda i, e: (0, e, 0, i))
    row = pl.BlockSpec((1, d), lambda i, e: (0, 0))
    tok = pl.BlockSpec((tt, d), lambda i, e: (i, 0))
    grouped = lambda a: a.reshape(PEER_HEADS, PEER_N_KEYS // PEER_IG, PEER_IG, t)
    s1, e1 = grouped(s1), grouped(e1)
    return pl.pallas_call(
        _peer_kernel,
        grid=(t // tt, ne),
        in_specs=[tok, pl.BlockSpec((PEER_E, d), lambda i, e: (e, 0)), pl.BlockSpec((d, PEER_E), lambda i, e: (0, e)),
                  grp, big, grp, big, pl.BlockSpec((PEER_HEADS, 8, tt), lambda i, e: (0, 0, i)), tok, row, row],
        out_specs=[tok, tok],
        out_shape=[jax.ShapeDtypeStruct((t, d), F32), jax.ShapeDtypeStruct((t, d), BF16)],
        scratch_shapes=[pltpu.VMEM((d, tt), F32), pltpu.VMEM((PEER_E, tt), BF16)],
        compiler_params=_params("parallel", "arbitrary"),
        name="peer_dense",
    )(hb, u_b, vt_b, s1, s2, e1, e2, tau, h, ln_g.reshape(1, d).astype(F32), ln_b.reshape(1, d).astype(F32))


def kernel(x, positions, w_in, gate_bias, rel_bias, conv_w, conv_b, lru_w_r, lru_b_r, lru_w_i, lru_b_i, lru_lambda,
           diff_lambda, diff_subln_g, w_branch, w_out, ln1_g, ln1_b, peer_w_q, peer_sub_keys, peer_u, peer_v,
           ln2_g, ln2_b):
    batch, seq, d = x.shape
    t = batch * seq
    h = x.reshape(t, d).astype(F32)
    hb = h.astype(BF16)
    pos_col = positions.reshape(t, 1).astype(jnp.int32)
    inv_freq = jnp.power(ROPE_THETA, -jnp.arange(0, HEAD_DIM, 2, dtype=F32) / HEAD_DIM)
    inv_tiled = jnp.tile(inv_freq, LANES // (HEAD_DIM // 2)).reshape(1, LANES)

    for l in range(DEPTH):
        lam_init = 0.8 - 0.6 * math.exp(-0.3 * l)
        z = _matmul(hb, w_in[l].astype(BF16), BF16)
        ya = _band_attention(z, _band_bias_table(rel_bias[l]), batch, seq)
        yb = _rglru(z, conv_w[l], conv_b[l], lru_w_r[l], lru_b_r[l], lru_w_i[l], lru_b_i[l], lru_lambda[l],
                    batch, seq)
        qr, kr = _rope(z, pos_col, inv_tiled)
        yc = _diff_attention(qr, kr, z, diff_lambda[l], diff_subln_g[l], lam_init, batch, seq)
        h, hb = _merge(ya, yb, yc, z, gate_bias[l], w_branch[l], w_out[l], h, ln1_g[l], ln1_b[l])
        s1, s2, e1, e2, tau = _peer_route(hb, peer_w_q[l].T.astype(BF16), peer_sub_keys[l].astype(BF16))
        h, hb = _peer_dense(hb, h, peer_u[l].astype(BF16), peer_v[l].T.astype(BF16), s1, s2, e1, e2, tau,
                            ln2_g[l], ln2_b[l])
    return h.reshape(batch, seq, d)
```

```python
import functools
import math

import numpy as np
import jax
import jax.numpy as jnp
from jax import lax
from jax.experimental import pallas as pl
from jax.experimental.pallas import tpu as pltpu

D_MODEL = 1024
DEPTH = 2
CHUNK = 64
HEAD_DIM = 64
A_HEADS = 8
A_LOOKBACK = 8
A_MAX_REL = 256
LRU_WIDTH = 512
LRU_BLOCKS = 8
LRU_BLOCK_DIM = 64
CONV_WIDTH = 4
LRU_C = 8.0
C_HEADS = 4
ROPE_THETA = 10000.0
N_BRANCHES = 3
BRANCH_WIDTH = 512
PEER_HEADS = 8
PEER_N_KEYS = 128
PEER_N_EXPERTS = PEER_N_KEYS * PEER_N_KEYS
PEER_HALF = 128
PEER_TOPK = 16
DEEPNORM_ALPHA = (2 * DEPTH) ** 0.25
LN_EPS = 1e-5
NEG_INF = -1e30

LANES = 128
VMEM_LIMIT = 56 * 1024 * 1024

COL_AQ, COL_AK, COL_AV = 0, 512, 1024
COL_BX, COL_BG = 1536, 2048
COL_CQ, COL_CK, COL_CV = 2560, 3072, 3584
COL_GL = 4096
IN_COLS = 7168

F32 = jnp.float32
BF16 = jnp.bfloat16
_NT = (((1,), (1,)), ((), ()))


def _params(*sem):
    return pltpu.CompilerParams(dimension_semantics=sem, vmem_limit_bytes=VMEM_LIMIT)


def _layer_norm(y, g, b):
    mu = jnp.mean(y, axis=-1, keepdims=True)
    d = y - mu
    var = jnp.mean(d * d, axis=-1, keepdims=True)
    return d * lax.rsqrt(var + LN_EPS) * g + b


def _mm_kernel(x_ref, w_ref, o_ref):
    o_ref[...] = jnp.dot(x_ref[...], w_ref[...], preferred_element_type=F32).astype(o_ref.dtype)


def _matmul(x, w, out_dtype, tm=512, tn=1024):
    m, k = x.shape
    n = w.shape[1]
    return pl.pallas_call(
        _mm_kernel,
        grid=(n // tn, m // tm),
        in_specs=[pl.BlockSpec((tm, k), lambda j, i: (i, 0)),
                  pl.BlockSpec((k, tn), lambda j, i: (0, j))],
        out_specs=pl.BlockSpec((tm, tn), lambda j, i: (i, j)),
        out_shape=jax.ShapeDtypeStruct((m, n), out_dtype),
        compiler_params=_params("parallel", "parallel"),
        name="in_proj",
    )(x, w)


def _rope_kernel(pos_ref, inv_ref, q_ref, k_ref, qo_ref, ko_ref):
    tt = pos_ref.shape[0]
    ang = pos_ref[...].astype(F32) * inv_ref[...]
    c = jnp.cos(ang)
    s = jnp.sin(ang)
    lane = lax.broadcasted_iota(jnp.int32, (tt, LANES), 1)
    first = (lane % HEAD_DIM) < (HEAD_DIM // 2)
    s_signed = jnp.where(first, -s, s)

    def rot(x):
        partner = jnp.where(first, pltpu.roll(x, LANES - HEAD_DIM // 2, 1), pltpu.roll(x, HEAD_DIM // 2, 1))
        return x * c + partner * s_signed

    for blk in range(q_ref.shape[1] // LANES):
        sl = slice(blk * LANES, (blk + 1) * LANES)
        qo_ref[:, sl] = (rot(q_ref[:, sl].astype(F32)) * (LOG2E * HEAD_DIM ** -0.5)).astype(qo_ref.dtype)
        ko_ref[:, sl] = rot(k_ref[:, sl].astype(F32)).astype(ko_ref.dtype)


def _rope(z, pos_col, inv_tiled, tt=512):
    t = z.shape[0]
    w = 512
    return pl.pallas_call(
        _rope_kernel,
        grid=(t // tt,),
        in_specs=[pl.BlockSpec((tt, 1), lambda i: (i, 0)),
                  pl.BlockSpec((1, LANES), lambda i: (0, 0)),
                  pl.BlockSpec((tt, w), lambda i: (i, COL_CQ // w)),
                  pl.BlockSpec((tt, w), lambda i: (i, COL_CK // w))],
        out_specs=[pl.BlockSpec((tt, w), lambda i: (i, 0)),
                   pl.BlockSpec((tt, w), lambda i: (i, 0))],
        out_shape=[jax.ShapeDtypeStruct((t, w), BF16), jax.ShapeDtypeStruct((t, w), BF16)],
        compiler_params=_params("parallel"),
        name="rope",
    )(pos_col, inv_tiled, z, z)


BAND_Q = 2 * CHUNK
BAND_K = (A_LOOKBACK + 2) * CHUNK
BAND_PAD = A_LOOKBACK * CHUNK


def _band_bias_table(rel_bias):
    rb = rel_bias.astype(F32)
    nh, rel = rb.shape
    far = jnp.broadcast_to(rb[:, -1:], (nh, BAND_PAD - A_MAX_REL))
    near = jnp.broadcast_to(rb[:, :1], (nh, BAND_K - (BAND_PAD - A_MAX_REL) - rel))
    wrap = jnp.broadcast_to(rb[:, -1:], (nh, BAND_Q))
    v = jnp.concatenate([far, rb[:, ::-1], near, wrap], axis=1)
    period = BAND_K + BAND_Q
    flat = jnp.tile(v, (1, BAND_Q))[:, :BAND_Q * (period - 1)]
    bias = flat.reshape(nh, BAND_Q, period - 1)[:, :, :BAND_K]
    qi = np.arange(BAND_Q)[:, None]
    kj = np.arange(BAND_K)[None, :]
    q_chunk = qi // CHUNK + A_LOOKBACK
    k_chunk = kj // CHUNK
    allowed = (k_chunk <= q_chunk) & (k_chunk >= q_chunk - A_LOOKBACK)
    return jnp.where(jnp.asarray(allowed)[None], bias, NEG_INF)


def _band_kernel(q_ref, k_ref, v_ref, tab_ref, o_ref, kp_ref, vp_ref):
    c = pl.program_id(2)

    @pl.when(c == 0)
    def _():
        zeros = jnp.zeros((BAND_PAD, LANES), kp_ref.dtype)
        kp_ref[0:BAND_PAD, :] = zeros
        vp_ref[0:BAND_PAD, :] = zeros
        kp_ref[BAND_PAD:, :] = k_ref[...]
        vp_ref[BAND_PAD:, :] = v_ref[...]

    start = pl.multiple_of(c * BAND_Q, BAND_Q)
    kw = kp_ref[pl.ds(start, BAND_K), :]
    vw = vp_ref[pl.ds(start, BAND_K), :]
    q = q_ref[...]
    lane = lax.broadcasted_iota(jnp.int32, (BAND_Q, LANES), 1)
    col = lax.broadcasted_iota(jnp.int32, (BAND_Q, BAND_K), 1)
    valid = (col + c * BAND_Q) >= BAND_PAD
    outs = []
    for hh in range(2):
        keep = (lane < HEAD_DIM) if hh == 0 else (lane >= HEAD_DIM)
        qm = jnp.where(keep, q, jnp.zeros_like(q))
        s = lax.dot_general(qm, kw, _NT, preferred_element_type=F32) * (HEAD_DIM ** -0.5)
        s = jnp.where(valid, s + tab_ref[hh], NEG_INF)
        m = jnp.max(s, axis=-1, keepdims=True)
        e = jnp.exp(s - m)
        p = e / jnp.sum(e, axis=-1, keepdims=True)
        outs.append(jnp.dot(p.astype(vw.dtype), vw, preferred_element_type=F32))
    o_ref[...] = jnp.where(lane < HEAD_DIM, outs[0], outs[1]).astype(o_ref.dtype)


def _band_attention(z, table, batch, seq):
    t = z.shape[0]
    nq = seq // BAND_Q
    return pl.pallas_call(
        _band_kernel,
        grid=(batch, A_HEADS // 2, nq),
        in_specs=[pl.BlockSpec((BAND_Q, LANES), lambda b, g, c: (b * nq + c, COL_AQ // LANES + g)),
                  pl.BlockSpec((seq, LANES), lambda b, g, c: (b, COL_AK // LANES + g)),
                  pl.BlockSpec((seq, LANES), lambda b, g, c: (b, COL_AV // LANES + g)),
                  pl.BlockSpec((2, BAND_Q, BAND_K), lambda b, g, c: (g, 0, 0))],
        out_specs=pl.BlockSpec((BAND_Q, LANES), lambda b, g, c: (b * nq + c, g)),
        out_shape=jax.ShapeDtypeStruct((t, A_HEADS * HEAD_DIM), BF16),
        scratch_shapes=[pltpu.VMEM((seq + BAND_PAD, LANES), z.dtype),
                        pltpu.VMEM((seq + BAND_PAD, LANES), z.dtype)],
        compiler_params=_params("parallel", "parallel", "arbitrary"),
        name="band_attn",
    )(z, z, z, table)


LRU_ROWS = 8


def _lru_kernel(bx_ref, bg_ref, cw_ref, cb_ref, wr_ref, br_ref, wi_ref, bi_ref, lam_ref, o_ref,
                tail_ref, h_ref, a_ref, u_ref):
    tt = bx_ref.shape[0]

    @pl.when(pl.program_id(1) == 0)
    def _():
        tail_ref[...] = jnp.zeros_like(tail_ref)
        h_ref[...] = jnp.zeros_like(h_ref)

    x = bx_ref[...].astype(F32)
    xx = jnp.concatenate([tail_ref[...], x], axis=0)
    tail_ref[...] = x[tt - 8:, :]
    xc = cb_ref[...] + xx[5:5 + tt, :] * cw_ref[0:1, :]
    for tap in range(1, CONV_WIDTH):
        xc = xc + xx[5 + tap:5 + tap + tt, :] * cw_ref[tap:tap + 1, :]
    xcb = xc.astype(BF16)
    r = jax.nn.sigmoid(jnp.dot(xcb, wr_ref[...], preferred_element_type=F32) + br_ref[...])
    i = jax.nn.sigmoid(jnp.dot(xcb, wi_ref[...], preferred_element_type=F32) + bi_ref[...])
    log_a = LRU_C * r * jax.nn.log_sigmoid(lam_ref[...])
    a = jnp.exp(log_a)
    a_ref[...] = a
    u_ref[...] = jnp.sqrt(1.0 - a * a) * (i * xc)

    def group(gi, h):
        base = pl.multiple_of(gi * LRU_ROWS, LRU_ROWS)
        for rr in range(LRU_ROWS):
            h = a_ref[pl.ds(base + rr, 1), :] * h + u_ref[pl.ds(base + rr, 1), :]
            u_ref[pl.ds(base + rr, 1), :] = h
        return h

    h_ref[...] = lax.fori_loop(0, tt // LRU_ROWS, group, h_ref[...])
    o_ref[...] = (u_ref[...] * jax.nn.gelu(bg_ref[...].astype(F32))).astype(o_ref.dtype)


def _block_diag(w):
    eye = jnp.eye(LRU_BLOCKS, dtype=w.dtype)
    return jnp.einsum('gij,gh->gihj', w, eye).reshape(LRU_WIDTH, LRU_WIDTH)


def _rglru(z, conv_w, conv_b, w_r, b_r, w_i, b_i, lru_lambda, batch, seq, tt=512):
    t = z.shape[0]
    nt = seq // tt
    w = LRU_WIDTH
    row = lambda a: a.reshape(1, w).astype(F32)
    full = lambda shape: pl.BlockSpec(shape, lambda b, j: (0,) * len(shape))
    return pl.pallas_call(
        _lru_kernel,
        grid=(batch, nt),
        in_specs=[pl.BlockSpec((tt, w), lambda b, j: (b * nt + j, COL_BX // w)),
                  pl.BlockSpec((tt, w), lambda b, j: (b * nt + j, COL_BG // w)),
                  full((CONV_WIDTH, w)), full((1, w)), full((w, w)), full((1, w)), full((w, w)), full((1, w)),
                  full((1, w))],
        out_specs=pl.BlockSpec((tt, w), lambda b, j: (b * nt + j, 0)),
        out_shape=jax.ShapeDtypeStruct((t, w), BF16),
        scratch_shapes=[pltpu.VMEM((8, w), F32), pltpu.VMEM((1, w), F32),
                        pltpu.VMEM((tt, w), F32), pltpu.VMEM((tt, w), F32)],
        compiler_params=_params("parallel", "arbitrary"),
        name="rglru",
    )(z, z, conv_w.astype(F32), row(conv_b), _block_diag(w_r).astype(BF16), row(b_r),
      _block_diag(w_i).astype(BF16), row(b_i), row(lru_lambda))


DIFF_TQ = 512
DIFF_TK = 512
LOG2E = 1.4426950408889634


def _diff_kernel(q_ref, k_ref, v_ref, dl_ref, g_ref, o_ref, m_ref, l_ref, acc_ref, *, lam_init):
    qi = pl.program_id(2)
    tq, tk = DIFF_TQ, DIFF_TK
    ng = tk // LANES
    q = q_ref[...]
    lane = lax.broadcasted_iota(jnp.int32, (tq, LANES), 1)
    qs = (jnp.where(lane < HEAD_DIM, q, jnp.zeros_like(q)), jnp.where(lane >= HEAD_DIM, q, jnp.zeros_like(q)))
    m_ref[...] = jnp.full(m_ref.shape, NEG_INF, F32)
    l_ref[...] = jnp.zeros(l_ref.shape, F32)
    acc_ref[...] = jnp.zeros(acc_ref.shape, F32)

    def block(j, masked):
        start = pl.multiple_of(j * tk, tk)
        kb = k_ref[pl.ds(start, tk), :]
        vb = v_ref[pl.ds(start, tk), :]
        if masked:
            row = lax.broadcasted_iota(jnp.int32, (tq, tk), 0)
            col = lax.broadcasted_iota(jnp.int32, (tq, tk), 1)
            ok = (col // CHUNK) <= (row // CHUNK)
        for mp in range(2):
            s = lax.dot_general(qs[mp], kb, _NT, preferred_element_type=F32)
            if masked:
                s = jnp.where(ok, s, NEG_INF)
            sg = [s[:, g * LANES:(g + 1) * LANES] for g in range(ng)]
            smax = functools.reduce(jnp.maximum, sg)
            m_old = m_ref[mp]
            m_new = jnp.maximum(m_old, jnp.max(smax, axis=-1, keepdims=True))
            alpha = jnp.exp2(m_old - m_new)
            pg = [jnp.exp2(x - m_new) for x in sg]
            l_ref[mp] = alpha * l_ref[mp] + functools.reduce(jnp.add, pg)
            p = jnp.concatenate(pg, axis=1).astype(vb.dtype)
            acc_ref[mp] = alpha * acc_ref[mp] + jnp.dot(p, vb, preferred_element_type=F32)
            m_ref[mp] = m_new

    def body(j, carry):
        block(j, False)
        return carry

    lax.fori_loop(0, qi, body, 0)
    block(qi, True)

    dl = dl_ref[...]
    lam = (jnp.exp(jnp.sum(dl[0:1] * dl[1:2], axis=-1, keepdims=True))
           - jnp.exp(jnp.sum(dl[2:3] * dl[3:4], axis=-1, keepdims=True)) + lam_init)
    l1 = jnp.sum(l_ref[0], axis=-1, keepdims=True)
    l2 = jnp.sum(l_ref[1], axis=-1, keepdims=True)
    o = acc_ref[0] / l1 - lam * (acc_ref[1] / l2)
    ms = jnp.mean(o * o, axis=-1, keepdims=True)
    o = o * lax.rsqrt(ms + LN_EPS) * g_ref[...] * (1.0 - lam_init)
    o_ref[...] = o.astype(o_ref.dtype)


def _diff_attention(qr, kr, z, diff_lambda, subln_g, lam_init, batch, seq):
    t = z.shape[0]
    nq = seq // DIFF_TQ
    return pl.pallas_call(
        functools.partial(_diff_kernel, lam_init=lam_init),
        grid=(batch, C_HEADS, nq),
        in_specs=[pl.BlockSpec((DIFF_TQ, LANES), lambda b, h, i: (b * nq + i, h)),
                  pl.BlockSpec((seq, LANES), lambda b, h, i: (b, h)),
                  pl.BlockSpec((seq, LANES), lambda b, h, i: (b, COL_CV // LANES + h)),
                  pl.BlockSpec((4, HEAD_DIM), lambda b, h, i: (0, 0)),
                  pl.BlockSpec((1, LANES), lambda b, h, i: (0, 0))],
        out_specs=pl.BlockSpec((DIFF_TQ, LANES), lambda b, h, i: (b * nq + i, h)),
        out_shape=jax.ShapeDtypeStruct((t, C_HEADS * 2 * HEAD_DIM), BF16),
        scratch_shapes=[pltpu.VMEM((2, DIFF_TQ, LANES), F32)] * 3,
        compiler_params=_params("parallel", "parallel", "arbitrary"),
        name="diff_attn",
    )(qr, kr, z, diff_lambda.astype(F32), subln_g.reshape(1, LANES).astype(F32))


def _merge_kernel(ya_ref, yb_ref, yc_ref, gl0_ref, gl1_ref, gl2_ref, gb_ref, wb_ref, wo_ref, h_ref, g_ref, b_ref,
                  o_ref, ob_ref):
    merged = None
    for n, (y_ref, gl_ref) in enumerate(((ya_ref, gl0_ref), (yb_ref, gl1_ref), (yc_ref, gl2_ref))):
        proj = jnp.dot(y_ref[...], wb_ref[n], preferred_element_type=F32)
        gate = jax.nn.sigmoid(gl_ref[...].astype(F32) + gb_ref[n:n + 1, :])
        merged = gate * proj if merged is None else merged + gate * proj
    mix = jnp.dot(merged.astype(BF16), wo_ref[...], preferred_element_type=F32)
    y = _layer_norm(DEEPNORM_ALPHA * h_ref[...] + mix, g_ref[...], b_ref[...])
    o_ref[...] = y
    ob_ref[...] = y.astype(ob_ref.dtype)


def _merge(ya, yb, yc, z, gate_bias, w_branch, w_out, h, ln_g, ln_b, tt=256):
    t = h.shape[0]
    d = D_MODEL
    bw = BRANCH_WIDTH
    ytile = pl.BlockSpec((tt, bw), lambda i: (i, 0))
    gl = lambda n: pl.BlockSpec((tt, d), lambda i: (i, COL_GL // d + n))
    full = lambda shape: pl.BlockSpec(shape, lambda i: (0,) * len(shape))
    return pl.pallas_call(
        _merge_kernel,
        grid=(t // tt,),
        in_specs=[ytile, ytile, ytile, gl(0), gl(1), gl(2), full((N_BRANCHES, d)), full((N_BRANCHES, bw, d)),
                  full((d, d)), pl.BlockSpec((tt, d), lambda i: (i, 0)), full((1, d)), full((1, d))],
        out_specs=[pl.BlockSpec((tt, d), lambda i: (i, 0)), pl.BlockSpec((tt, d), lambda i: (i, 0))],
        out_shape=[jax.ShapeDtypeStruct((t, d), F32), jax.ShapeDtypeStruct((t, d), BF16)],
        compiler_params=_params("parallel"),
        name="merge_out_ln",
    )(ya, yb, yc, z, z, z, gate_bias.astype(F32), w_branch.astype(BF16), w_out.astype(BF16), h,
      ln_g.reshape(1, d).astype(F32), ln_b.reshape(1, d).astype(F32))


def _top_rows(s, k, with_rank=False):
    rows = []
    cur = s
    rank = jnp.full(s.shape, float(k), F32) if with_rank else None
    for r in range(k):
        m = jnp.max(cur, axis=0, keepdims=True)
        rows.append(m)
        hit = cur == m
        if with_rank:
            rank = jnp.where(hit, float(r), rank)
        cur = jnp.where(hit, -jnp.inf, cur)
    return (rows, rank) if with_rank else rows


def _pair_word(x):
    bits = lax.bitcast_convert_type(x.astype(BF16).astype(F32), jnp.uint32)
    return bits | (bits >> 16)


def _route_kernel(h_ref, wq_ref, keys_ref, rk_ref, e1_ref, cnt_ref, e2_ref):
    qt = lax.dot_general(wq_ref[...], h_ref[...], _NT, preferred_element_type=F32).astype(BF16)
    for h in range(PEER_HEADS):
        sc = []
        for p in range(2):
            r0 = (h * 2 + p) * PEER_HALF
            sc.append(jnp.dot(keys_ref[h, p], qt[r0:r0 + PEER_HALF, :], preferred_element_type=F32))
        s1, s2 = sc
        a, rank1 = _top_rows(s1, PEER_TOPK, with_rank=True)
        b = _top_rows(s2, PEER_TOPK)
        cand = jnp.concatenate([a[k] + b[l] for k in range(PEER_TOPK) for l in range(PEER_TOPK // (k + 1))], axis=0)
        tau = _top_rows(cand, PEER_TOPK)[PEER_TOPK - 1]
        top = a[0] + b[0]
        zsum = jnp.sum(jnp.where(cand >= tau, jnp.exp(cand - top), 0.0), axis=0, keepdims=True)
        cnt = jnp.zeros(s2.shape, F32)
        for k in range(PEER_TOPK):
            cnt = cnt + jnp.where(a[k] + s2 >= tau, 1.0, 0.0)
        rk_ref[h] = _pair_word(rank1)
        e1_ref[h] = _pair_word(jnp.exp(s1 - a[0]) / zsum)
        cnt_ref[h] = cnt.astype(cnt_ref.dtype)
        e2_ref[h] = jnp.exp(s2 - b[0]).astype(e2_ref.dtype)


def _peer_route(hb, wq_t, keys, tt=256):
    t = hb.shape[0]
    shape = (PEER_HEADS, PEER_N_KEYS, t)
    bspec = pl.BlockSpec((PEER_HEADS, PEER_N_KEYS, tt), lambda i: (0, 0, i))
    return pl.pallas_call(
        _route_kernel,
        grid=(t // tt,),
        in_specs=[pl.BlockSpec((tt, D_MODEL), lambda i: (i, 0)),
                  pl.BlockSpec(wq_t.shape, lambda i: (0, 0)),
                  pl.BlockSpec(keys.shape, lambda i: (0, 0, 0, 0))],
        out_specs=[bspec, bspec, bspec, bspec],
        out_shape=[jax.ShapeDtypeStruct(shape, jnp.uint32), jax.ShapeDtypeStruct(shape, jnp.uint32),
                   jax.ShapeDtypeStruct(shape, BF16), jax.ShapeDtypeStruct(shape, BF16)],
        compiler_params=_params("parallel"),
        name="peer_route",
    )(hb, wq_t, keys)


PEER_TT = 512
PEER_IG = 8
PEER_EB = PEER_IG * PEER_N_KEYS
PEER_NB = 2
PEER_E = PEER_NB * PEER_EB
PEER_JB = 64
GELU_C1 = -2.0 * math.sqrt(2.0 / math.pi) * LOG2E
GELU_C2 = GELU_C1 * 0.044715


def _gelu_tanh(x):
    return x * (1.0 / (1.0 + jnp.exp2(x * (GELU_C1 + GELU_C2 * (x * x)))))


def _peer_kernel(hb_ref, u_ref, vt_ref, rk_ref, cnt_ref, e1_ref, e2_ref, h_ref, g_ref, b_ref,
                 o_ref, ob_ref, acc_ref, p_ref):
    e = pl.program_id(1)
    tt = hb_ref.shape[0]

    @pl.when(e == 0)
    def _():
        acc_ref[...] = jnp.zeros_like(acc_ref)

    def act_block(k):
        a = lax.dot_general(u_ref[k * PEER_EB:(k + 1) * PEER_EB, :], hb_ref[...], _NT, preferred_element_type=F32)
        p_ref[k] = _gelu_tanh(a).astype(p_ref.dtype)

    def row_pairs(ref, h, k, ii, ls):
        return pltpu.bitcast(jnp.broadcast_to(ref[h, k, ii:ii + 1, ls], (PEER_JB // 2, LANES)), BF16)

    def gate_block(k):
        for lc in range(tt // LANES):
            ls = slice(lc * LANES, (lc + 1) * LANES)
            for jb in range(PEER_N_KEYS // PEER_JB):
                js = slice(jb * PEER_JB, (jb + 1) * PEER_JB)
                w = [jnp.zeros((PEER_JB, LANES), BF16) for _ in range(PEER_IG)]
                for h in range(PEER_HEADS):
                    cnt = cnt_ref[h, js, ls]
                    e2 = e2_ref[h, js, ls]
                    for ii in range(PEER_IG):
                        gate = row_pairs(e1_ref, h, k, ii, ls) * e2
                        w[ii] = jnp.where(row_pairs(rk_ref, h, k, ii, ls) < cnt, w[ii] + gate, w[ii])
                for ii in range(PEER_IG):
                    rs = slice(ii * PEER_N_KEYS + jb * PEER_JB, ii * PEER_N_KEYS + (jb + 1) * PEER_JB)
                    p_ref[k, rs, ls] = w[ii] * p_ref[k, rs, ls]

    act_block(0)
    for k in range(PEER_NB):
        if k + 1 < PEER_NB:
            act_block(k + 1)
        gate_block(k)
        acc_ref[...] += jnp.dot(vt_ref[:, k * PEER_EB:(k + 1) * PEER_EB], p_ref[k], preferred_element_type=F32)

    @pl.when(e == pl.num_programs(1) - 1)
    def _():
        ff = acc_ref[...].T
        y = _layer_norm(DEEPNORM_ALPHA * h_ref[...] + ff, g_ref[...], b_ref[...])
        o_ref[...] = y
        ob_ref[...] = y.astype(ob_ref.dtype)


def _peer_dense(hb, h, u_b, vt_b, rk, e1, cnt, e2, ln_g, ln_b):
    t = h.shape[0]
    d = D_MODEL
    tt = min(PEER_TT, t)
    ne = PEER_N_EXPERTS // PEER_E
    big = pl.BlockSpec((PEER_HEADS, PEER_N_KEYS, tt), lambda i, e: (0, 0, i))
    grp = pl.BlockSpec((PEER_HEADS, PEER_NB, PEER_IG, tt), lambda i, e: (0, e, 0, i))
    row = pl.BlockSpec((1, d), lambda i, e: (0, 0))
    tok = pl.BlockSpec((tt, d), lambda i, e: (i, 0))
    grouped = lambda a: a.reshape(PEER_HEADS, PEER_N_KEYS // PEER_IG, PEER_IG, t)
    return pl.pallas_call(
        _peer_kernel,
        grid=(t // tt, ne),
        in_specs=[tok, pl.BlockSpec((PEER_E, d), lambda i, e: (e, 0)), pl.BlockSpec((d, PEER_E), lambda i, e: (0, e)),
                  grp, big, grp, big, tok, row, row],
        out_specs=[tok, tok],
        out_shape=[jax.ShapeDtypeStruct((t, d), F32), jax.ShapeDtypeStruct((t, d), BF16)],
        scratch_shapes=[pltpu.VMEM((d, tt), F32), pltpu.VMEM((PEER_NB, PEER_EB, tt), BF16)],
        compiler_params=_params("parallel", "arbitrary"),
        name="peer_dense",
    )(hb, u_b, vt_b, grouped(rk), cnt, grouped(e1), e2, h, ln_g.reshape(1, d).astype(F32),
      ln_b.reshape(1, d).astype(F32))


def kernel(x, positions, w_in, gate_bias, rel_bias, conv_w, conv_b, lru_w_r, lru_b_r, lru_w_i, lru_b_i, lru_lambda,
           diff_lambda, diff_subln_g, w_branch, w_out, ln1_g, ln1_b, peer_w_q, peer_sub_keys, peer_u, peer_v,
           ln2_g, ln2_b):
    batch, seq, d = x.shape
    t = batch * seq
    h = x.reshape(t, d).astype(F32)
    hb = h.astype(BF16)
    pos_col = positions.reshape(t, 1).astype(jnp.int32)
    inv_freq = jnp.power(ROPE_THETA, -jnp.arange(0, HEAD_DIM, 2, dtype=F32) / HEAD_DIM)
    inv_tiled = jnp.tile(inv_freq, LANES // (HEAD_DIM // 2)).reshape(1, LANES)

    for l in range(DEPTH):
        lam_init = 0.8 - 0.6 * math.exp(-0.3 * l)
        z = _matmul(hb, w_in[l].astype(BF16), BF16)
        ya = _band_attention(z, _band_bias_table(rel_bias[l]), batch, seq)
        yb = _rglru(z, conv_w[l], conv_b[l], lru_w_r[l], lru_b_r[l], lru_w_i[l], lru_b_i[l], lru_lambda[l],
                    batch, seq)
        qr, kr = _rope(z, pos_col, inv_tiled)
        yc = _diff_attention(qr, kr, z, diff_lambda[l], diff_subln_g[l], lam_init, batch, seq)
        h, hb = _merge(ya, yb, yc, z, gate_bias[l], w_branch[l], w_out[l], h, ln1_g[l], ln1_b[l])
        rk, e1, cnt, e2 = _peer_route(hb, peer_w_q[l].T.astype(BF16), peer_sub_keys[l].astype(BF16))
        h, hb = _peer_dense(hb, h, peer_u[l].astype(BF16), peer_v[l].T.astype(BF16), rk, e1, cnt, e2,
                            ln2_g[l], ln2_b[l])
    return h.reshape(batch, seq, d)
```

```python
import functools
import math

import numpy as np
import jax
import jax.numpy as jnp
from jax import lax
from jax.experimental import pallas as pl
from jax.experimental.pallas import tpu as pltpu

D_MODEL = 1024
DEPTH = 2
CHUNK = 64
HEAD_DIM = 64
A_HEADS = 8
A_LOOKBACK = 8
A_MAX_REL = 256
LRU_WIDTH = 512
LRU_BLOCKS = 8
LRU_BLOCK_DIM = 64
CONV_WIDTH = 4
LRU_C = 8.0
C_HEADS = 4
ROPE_THETA = 10000.0
N_BRANCHES = 3
BRANCH_WIDTH = 512
PEER_HEADS = 8
PEER_N_KEYS = 128
PEER_N_EXPERTS = PEER_N_KEYS * PEER_N_KEYS
PEER_HALF = 128
PEER_TOPK = 16
DEEPNORM_ALPHA = (2 * DEPTH) ** 0.25
LN_EPS = 1e-5
NEG_INF = -1e30

LANES = 128
VMEM_LIMIT = 56 * 1024 * 1024

COL_AQ, COL_AK, COL_AV = 0, 512, 1024
COL_BX, COL_BG = 1536, 2048
COL_CQ, COL_CK, COL_CV = 2560, 3072, 3584
COL_GL = 4096
IN_COLS = 7168

F32 = jnp.float32
BF16 = jnp.bfloat16
_NT = (((1,), (1,)), ((), ()))


def _params(*sem):
    return pltpu.CompilerParams(dimension_semantics=sem, vmem_limit_bytes=VMEM_LIMIT)


def _layer_norm(y, g, b):
    mu = jnp.mean(y, axis=-1, keepdims=True)
    d = y - mu
    var = jnp.mean(d * d, axis=-1, keepdims=True)
    return d * lax.rsqrt(var + LN_EPS) * g + b


def _mm_kernel(x_ref, w_ref, o_ref):
    o_ref[...] = jnp.dot(x_ref[...], w_ref[...], preferred_element_type=F32).astype(o_ref.dtype)


def _matmul(x, w, out_dtype, tm=1024, tn=1024):
    m, k = x.shape
    n = w.shape[1]
    return pl.pallas_call(
        _mm_kernel,
        grid=(n // tn, m // tm),
        in_specs=[pl.BlockSpec((tm, k), lambda j, i: (i, 0)),
                  pl.BlockSpec((k, tn), lambda j, i: (0, j))],
        out_specs=pl.BlockSpec((tm, tn), lambda j, i: (i, j)),
        out_shape=jax.ShapeDtypeStruct((m, n), out_dtype),
        compiler_params=_params("parallel", "parallel"),
        name="in_proj",
    )(x, w)


def _rope_kernel(pos_ref, inv_ref, q_ref, k_ref, qo_ref, ko_ref):
    tt = pos_ref.shape[0]
    ang = pos_ref[...].astype(F32) * inv_ref[...]
    c = jnp.cos(ang)
    s = jnp.sin(ang)
    lane = lax.broadcasted_iota(jnp.int32, (tt, LANES), 1)
    first = (lane % HEAD_DIM) < (HEAD_DIM // 2)
    s_signed = jnp.where(first, -s, s)

    def rot(x):
        partner = jnp.where(first, pltpu.roll(x, LANES - HEAD_DIM // 2, 1), pltpu.roll(x, HEAD_DIM // 2, 1))
        return x * c + partner * s_signed

    for blk in range(q_ref.shape[1] // LANES):
        sl = slice(blk * LANES, (blk + 1) * LANES)
        qo_ref[:, sl] = (rot(q_ref[:, sl].astype(F32)) * (LOG2E * HEAD_DIM ** -0.5)).astype(qo_ref.dtype)
        ko_ref[:, sl] = rot(k_ref[:, sl].astype(F32)).astype(ko_ref.dtype)


def _rope(z, pos_col, inv_tiled, tt=512):
    t = z.shape[0]
    w = 512
    return pl.pallas_call(
        _rope_kernel,
        grid=(t // tt,),
        in_specs=[pl.BlockSpec((tt, 1), lambda i: (i, 0)),
                  pl.BlockSpec((1, LANES), lambda i: (0, 0)),
                  pl.BlockSpec((tt, w), lambda i: (i, COL_CQ // w)),
                  pl.BlockSpec((tt, w), lambda i: (i, COL_CK // w))],
        out_specs=[pl.BlockSpec((tt, w), lambda i: (i, 0)),
                   pl.BlockSpec((tt, w), lambda i: (i, 0))],
        out_shape=[jax.ShapeDtypeStruct((t, w), BF16), jax.ShapeDtypeStruct((t, w), BF16)],
        compiler_params=_params("parallel"),
        name="rope",
    )(pos_col, inv_tiled, z, z)


BAND_Q = 2 * CHUNK
BAND_K = (A_LOOKBACK + 2) * CHUNK
BAND_PAD = A_LOOKBACK * CHUNK


def _band_bias_table(rel_bias):
    rb = rel_bias.astype(F32)
    nh, rel = rb.shape
    far = jnp.broadcast_to(rb[:, -1:], (nh, BAND_PAD - A_MAX_REL))
    near = jnp.broadcast_to(rb[:, :1], (nh, BAND_K - (BAND_PAD - A_MAX_REL) - rel))
    wrap = jnp.broadcast_to(rb[:, -1:], (nh, BAND_Q))
    v = jnp.concatenate([far, rb[:, ::-1], near, wrap], axis=1)
    period = BAND_K + BAND_Q
    flat = jnp.tile(v, (1, BAND_Q))[:, :BAND_Q * (period - 1)]
    bias = flat.reshape(nh, BAND_Q, period - 1)[:, :, :BAND_K]
    qi = np.arange(BAND_Q)[:, None]
    kj = np.arange(BAND_K)[None, :]
    q_chunk = qi // CHUNK + A_LOOKBACK
    k_chunk = kj // CHUNK
    allowed = (k_chunk <= q_chunk) & (k_chunk >= q_chunk - A_LOOKBACK)
    return jnp.where(jnp.asarray(allowed)[None], bias, NEG_INF)


def _band_kernel(q_ref, k_ref, v_ref, tab_ref, o_ref, kp_ref, vp_ref):
    c = pl.program_id(2)

    @pl.when(c == 0)
    def _():
        zeros = jnp.zeros((BAND_PAD, LANES), kp_ref.dtype)
        kp_ref[0:BAND_PAD, :] = zeros
        vp_ref[0:BAND_PAD, :] = zeros
        kp_ref[BAND_PAD:, :] = k_ref[...]
        vp_ref[BAND_PAD:, :] = v_ref[...]

    lane = lax.broadcasted_iota(jnp.int32, (BAND_Q, LANES), 1)
    col = lax.broadcasted_iota(jnp.int32, (BAND_Q, BAND_K), 1)
    for blk in range(BAND_NBLK):
        first = (c * BAND_NBLK + blk) * BAND_Q
        start = pl.multiple_of(first, BAND_Q)
        kw = kp_ref[pl.ds(start, BAND_K), :]
        vw = vp_ref[pl.ds(start, BAND_K), :]
        rows = slice(blk * BAND_Q, (blk + 1) * BAND_Q)
        q = q_ref[rows, :]
        valid = (col + first) >= BAND_PAD
        outs = []
        for hh in range(2):
            keep = (lane < HEAD_DIM) if hh == 0 else (lane >= HEAD_DIM)
            qm = jnp.where(keep, q, jnp.zeros_like(q))
            s = lax.dot_general(qm, kw, _NT, preferred_element_type=F32) * (HEAD_DIM ** -0.5)
            s = jnp.where(valid, s + tab_ref[hh], NEG_INF)
            m = jnp.max(s, axis=-1, keepdims=True)
            e = jnp.exp(s - m)
            p = e / jnp.sum(e, axis=-1, keepdims=True)
            outs.append(jnp.dot(p.astype(vw.dtype), vw, preferred_element_type=F32))
        o_ref[rows, :] = jnp.where(lane < HEAD_DIM, outs[0], outs[1]).astype(o_ref.dtype)


BAND_NBLK = 4


def _band_attention(z, table, batch, seq):
    t = z.shape[0]
    nq = seq // (BAND_Q * BAND_NBLK)
    return pl.pallas_call(
        _band_kernel,
        grid=(batch, A_HEADS // 2, nq),
        in_specs=[pl.BlockSpec((BAND_Q * BAND_NBLK, LANES), lambda b, g, c: (b * nq + c, COL_AQ // LANES + g)),
                  pl.BlockSpec((seq, LANES), lambda b, g, c: (b, COL_AK // LANES + g)),
                  pl.BlockSpec((seq, LANES), lambda b, g, c: (b, COL_AV // LANES + g)),
                  pl.BlockSpec((2, BAND_Q, BAND_K), lambda b, g, c: (g, 0, 0))],
        out_specs=pl.BlockSpec((BAND_Q * BAND_NBLK, LANES), lambda b, g, c: (b * nq + c, g)),
        out_shape=jax.ShapeDtypeStruct((t, A_HEADS * HEAD_DIM), BF16),
        scratch_shapes=[pltpu.VMEM((seq + BAND_PAD, LANES), z.dtype),
                        pltpu.VMEM((seq + BAND_PAD, LANES), z.dtype)],
        compiler_params=_params("parallel", "parallel", "arbitrary"),
        name="band_attn",
    )(z, z, z, table)


LRU_ROWS = 8


def _lru_kernel(bx_ref, bg_ref, cw_ref, cb_ref, wr_ref, br_ref, wi_ref, bi_ref, lam_ref, o_ref,
                tail_ref, h_ref, a_ref, u_ref):
    tt = bx_ref.shape[0]

    @pl.when(pl.program_id(1) == 0)
    def _():
        tail_ref[...] = jnp.zeros_like(tail_ref)
        h_ref[...] = jnp.zeros_like(h_ref)

    x = bx_ref[...].astype(F32)
    xx = jnp.concatenate([tail_ref[...], x], axis=0)
    tail_ref[...] = x[tt - 8:, :]
    xc = cb_ref[...] + xx[5:5 + tt, :] * cw_ref[0:1, :]
    for tap in range(1, CONV_WIDTH):
        xc = xc + xx[5 + tap:5 + tap + tt, :] * cw_ref[tap:tap + 1, :]
    xcb = xc.astype(BF16)
    r = jax.nn.sigmoid(jnp.dot(xcb, wr_ref[...], preferred_element_type=F32) + br_ref[...])
    i = jax.nn.sigmoid(jnp.dot(xcb, wi_ref[...], preferred_element_type=F32) + bi_ref[...])
    log_a = LRU_C * r * jax.nn.log_sigmoid(lam_ref[...])
    a = jnp.exp(log_a)
    a_ref[...] = a
    u_ref[...] = jnp.sqrt(1.0 - a * a) * (i * xc)

    def group(gi, h):
        base = pl.multiple_of(gi * LRU_ROWS, LRU_ROWS)
        for rr in range(LRU_ROWS):
            h = a_ref[pl.ds(base + rr, 1), :] * h + u_ref[pl.ds(base + rr, 1), :]
            u_ref[pl.ds(base + rr, 1), :] = h
        return h

    h_ref[...] = lax.fori_loop(0, tt // LRU_ROWS, group, h_ref[...])
    o_ref[...] = (u_ref[...] * jax.nn.gelu(bg_ref[...].astype(F32))).astype(o_ref.dtype)


def _block_diag(w):
    eye = jnp.eye(LRU_BLOCKS, dtype=w.dtype)
    return jnp.einsum('gij,gh->gihj', w, eye).reshape(LRU_WIDTH, LRU_WIDTH)


def _rglru(z, conv_w, conv_b, w_r, b_r, w_i, b_i, lru_lambda, batch, seq, tt=512):
    t = z.shape[0]
    nt = seq // tt
    w = LRU_WIDTH
    row = lambda a: a.reshape(1, w).astype(F32)
    full = lambda shape: pl.BlockSpec(shape, lambda b, j: (0,) * len(shape))
    return pl.pallas_call(
        _lru_kernel,
        grid=(batch, nt),
        in_specs=[pl.BlockSpec((tt, w), lambda b, j: (b * nt + j, COL_BX // w)),
                  pl.BlockSpec((tt, w), lambda b, j: (b * nt + j, COL_BG // w)),
                  full((CONV_WIDTH, w)), full((1, w)), full((w, w)), full((1, w)), full((w, w)), full((1, w)),
                  full((1, w))],
        out_specs=pl.BlockSpec((tt, w), lambda b, j: (b * nt + j, 0)),
        out_shape=jax.ShapeDtypeStruct((t, w), BF16),
        scratch_shapes=[pltpu.VMEM((8, w), F32), pltpu.VMEM((1, w), F32),
                        pltpu.VMEM((tt, w), F32), pltpu.VMEM((tt, w), F32)],
        compiler_params=_params("parallel", "arbitrary"),
        name="rglru",
    )(z, z, conv_w.astype(F32), row(conv_b), _block_diag(w_r).astype(BF16), row(b_r),
      _block_diag(w_i).astype(BF16), row(b_i), row(lru_lambda))


DIFF_TQ = 1024
DIFF_TK = 1024
LOG2E = 1.4426950408889634
DIFF_NSUB = 2


def _diff_kernel(q_ref, k_ref, v_ref, dl_ref, g_ref, o_ref, m_ref, l_ref, acc_ref, *, lam_init):
    qi = pl.program_id(2)
    tq, tk = DIFF_TQ, DIFF_TK
    q = q_ref[...]
    lane = lax.broadcasted_iota(jnp.int32, (tq, LANES), 1)
    qs = (jnp.where(lane < HEAD_DIM, q, jnp.zeros_like(q)), jnp.where(lane >= HEAD_DIM, q, jnp.zeros_like(q)))
    m_ref[...] = jnp.full(m_ref.shape, NEG_INF, F32)
    l_ref[...] = jnp.zeros(l_ref.shape, F32)
    acc_ref[...] = jnp.zeros(acc_ref.shape, F32)

    def block(j, masked):
        start = pl.multiple_of(j * tk, tk)
        kb = k_ref[pl.ds(start, tk), :]
        vtb = v_ref[j]
        chains = [(sub, mp) for sub in range(DIFF_NSUB) for mp in range(2)]
        tsub = tq // DIFF_NSUB
        scores = [lax.dot_general(kb, qs[mp][sub * tsub:(sub + 1) * tsub], _NT, preferred_element_type=F32)
                  for sub, mp in chains]
        for (sub, mp), s in zip(chains, scores):
            cols = slice(sub * tsub, (sub + 1) * tsub)
            if masked:
                key = lax.broadcasted_iota(jnp.int32, (tk, tsub), 0)
                qry = lax.broadcasted_iota(jnp.int32, (tk, tsub), 1) + sub * tsub
                s = jnp.where((key // CHUNK) <= (qry // CHUNK), s, NEG_INF)
            m_old = m_ref[mp, :, cols]
            m_new = jnp.maximum(m_old, jnp.max(s, axis=0, keepdims=True))
            alpha = jnp.exp2(m_old - m_new)
            p = jnp.exp2(s - m_new)
            l_ref[mp, :, cols] = alpha * l_ref[mp, :, cols] + jnp.sum(p, axis=0, keepdims=True)
            acc_ref[mp, :, cols] = alpha * acc_ref[mp, :, cols] + jnp.dot(vtb, p.astype(vtb.dtype),
                                                                          preferred_element_type=F32)
            m_ref[mp, :, cols] = m_new

    def body(j, carry):
        block(j, False)
        return carry

    lax.fori_loop(0, qi, body, 0)
    block(qi, True)

    dl = dl_ref[...]
    lam = (jnp.exp(jnp.sum(dl[0:1] * dl[1:2], axis=-1, keepdims=True))
           - jnp.exp(jnp.sum(dl[2:3] * dl[3:4], axis=-1, keepdims=True)) + lam_init)
    o = acc_ref[0] / l_ref[0] - lam * (acc_ref[1] / l_ref[1])
    ms = jnp.mean(o * o, axis=0, keepdims=True)
    o = o * lax.rsqrt(ms + LN_EPS) * g_ref[...] * (1.0 - lam_init)
    o_ref[...] = o.T.astype(o_ref.dtype)


def _diff_attention(qr, kr, z, diff_lambda, subln_g, lam_init, batch, seq):
    t = z.shape[0]
    nq = seq // DIFF_TQ
    nk = seq // DIFF_TK
    hd = 2 * HEAD_DIM
    vt = z[:, COL_CV:COL_CV + C_HEADS * hd].reshape(t // DIFF_TK, DIFF_TK, C_HEADS * hd).transpose(0, 2, 1)
    return pl.pallas_call(
        functools.partial(_diff_kernel, lam_init=lam_init),
        grid=(batch, C_HEADS, nq),
        in_specs=[pl.BlockSpec((DIFF_TQ, LANES), lambda b, h, i: (b * nq + i, h)),
                  pl.BlockSpec((seq, LANES), lambda b, h, i: (b, h)),
                  pl.BlockSpec((nk, hd, DIFF_TK), lambda b, h, i: (b, h, 0)),
                  pl.BlockSpec((4, HEAD_DIM), lambda b, h, i: (0, 0)),
                  pl.BlockSpec((hd, 1), lambda b, h, i: (0, 0))],
        out_specs=pl.BlockSpec((DIFF_TQ, LANES), lambda b, h, i: (b * nq + i, h)),
        out_shape=jax.ShapeDtypeStruct((t, C_HEADS * hd), BF16),
        scratch_shapes=[pltpu.VMEM((2, 1, DIFF_TQ), F32), pltpu.VMEM((2, 1, DIFF_TQ), F32),
                        pltpu.VMEM((2, hd, DIFF_TQ), F32)],
        compiler_params=_params("parallel", "parallel", "arbitrary"),
        name="diff_attn",
    )(qr, kr, vt, diff_lambda.astype(F32), subln_g.reshape(hd, 1).astype(F32))


def _merge_kernel(ya_ref, yb_ref, yc_ref, gl0_ref, gl1_ref, gl2_ref, gb_ref, wb_ref, wo_ref, h_ref, g_ref, b_ref,
                  o_ref, ob_ref):
    merged = None
    for n, (y_ref, gl_ref) in enumerate(((ya_ref, gl0_ref), (yb_ref, gl1_ref), (yc_ref, gl2_ref))):
        proj = jnp.dot(y_ref[...], wb_ref[n], preferred_element_type=F32)
        gate = jax.nn.sigmoid(gl_ref[...].astype(F32) + gb_ref[n:n + 1, :])
        merged = gate * proj if merged is None else merged + gate * proj
    mix = jnp.dot(merged.astype(BF16), wo_ref[...], preferred_element_type=F32)
    y = _layer_norm(DEEPNORM_ALPHA * h_ref[...] + mix, g_ref[...], b_ref[...])
    o_ref[...] = y
    ob_ref[...] = y.astype(ob_ref.dtype)


def _merge(ya, yb, yc, z, gate_bias, w_branch, w_out, h, ln_g, ln_b, tt=256):
    t = h.shape[0]
    d = D_MODEL
    bw = BRANCH_WIDTH
    ytile = pl.BlockSpec((tt, bw), lambda i: (i, 0))
    gl = lambda n: pl.BlockSpec((tt, d), lambda i: (i, COL_GL // d + n))
    full = lambda shape: pl.BlockSpec(shape, lambda i: (0,) * len(shape))
    return pl.pallas_call(
        _merge_kernel,
        grid=(t // tt,),
        in_specs=[ytile, ytile, ytile, gl(0), gl(1), gl(2), full((N_BRANCHES, d)), full((N_BRANCHES, bw, d)),
                  full((d, d)), pl.BlockSpec((tt, d), lambda i: (i, 0)), full((1, d)), full((1, d))],
        out_specs=[pl.BlockSpec((tt, d), lambda i: (i, 0)), pl.BlockSpec((tt, d), lambda i: (i, 0))],
        out_shape=[jax.ShapeDtypeStruct((t, d), F32), jax.ShapeDtypeStruct((t, d), BF16)],
        compiler_params=_params("parallel"),
        name="merge_out_ln",
    )(ya, yb, yc, z, z, z, gate_bias.astype(F32), w_branch.astype(BF16), w_out.astype(BF16), h,
      ln_g.reshape(1, d).astype(F32), ln_b.reshape(1, d).astype(F32))


def _top_rows(s, k, with_rank=False):
    rows = []
    cur = s
    rank = jnp.full(s.shape, float(k), F32) if with_rank else None
    for r in range(k):
        m = jnp.max(cur, axis=0, keepdims=True)
        rows.append(m)
        hit = cur == m
        if with_rank:
            rank = jnp.where(hit, float(r), rank)
        cur = jnp.where(hit, -jnp.inf, cur)
    return (rows, rank) if with_rank else rows


def _pair_word(x):
    bits = lax.bitcast_convert_type(x.astype(BF16).astype(F32), jnp.uint32)
    return bits | (bits >> 16)


def _route_kernel(h_ref, wq_ref, keys_ref, rk_ref, e1_ref, cnt_ref, e2_ref):
    qt = lax.dot_general(wq_ref[...], h_ref[...], _NT, preferred_element_type=F32).astype(BF16)
    for h in range(PEER_HEADS):
        sc = []
        for p in range(2):
            r0 = (h * 2 + p) * PEER_HALF
            sc.append(jnp.dot(keys_ref[h, p], qt[r0:r0 + PEER_HALF, :], preferred_element_type=F32))
        s1, s2 = sc
        a, rank1 = _top_rows(s1, PEER_TOPK, with_rank=True)
        b = _top_rows(s2, PEER_TOPK)
        cand = jnp.concatenate([a[k] + b[l] for k in range(PEER_TOPK) for l in range(PEER_TOPK // (k + 1))], axis=0)
        tau = _top_rows(cand, PEER_TOPK)[PEER_TOPK - 1]
        top = a[0] + b[0]
        zsum = jnp.sum(jnp.where(cand >= tau, jnp.exp(cand - top), 0.0), axis=0, keepdims=True)
        cnt = jnp.zeros(s2.shape, F32)
        for k in range(PEER_TOPK):
            cnt = cnt + jnp.where(a[k] + s2 >= tau, 1.0, 0.0)
        rk_ref[h] = _pair_word(rank1)
        e1_ref[h] = _pair_word(jnp.exp(s1 - a[0]) / zsum)
        cnt_ref[h] = cnt.astype(cnt_ref.dtype)
        e2_ref[h] = jnp.exp(s2 - b[0]).astype(e2_ref.dtype)


def _peer_route(hb, wq_t, keys, tt=256):
    t = hb.shape[0]
    shape = (PEER_HEADS, PEER_N_KEYS, t)
    bspec = pl.BlockSpec((PEER_HEADS, PEER_N_KEYS, tt), lambda i: (0, 0, i))
    return pl.pallas_call(
        _route_kernel,
        grid=(t // tt,),
        in_specs=[pl.BlockSpec((tt, D_MODEL), lambda i: (i, 0)),
                  pl.BlockSpec(wq_t.shape, lambda i: (0, 0)),
                  pl.BlockSpec(keys.shape, lambda i: (0, 0, 0, 0))],
        out_specs=[bspec, bspec, bspec, bspec],
        out_shape=[jax.ShapeDtypeStruct(shape, jnp.uint32), jax.ShapeDtypeStruct(shape, jnp.uint32),
                   jax.ShapeDtypeStruct(shape, BF16), jax.ShapeDtypeStruct(shape, BF16)],
        compiler_params=_params("parallel"),
        name="peer_route",
    )(hb, wq_t, keys)


PEER_TT = 512
PEER_IG = 8
PEER_EB = PEER_IG * PEER_N_KEYS
PEER_NB = 2
PEER_E = PEER_NB * PEER_EB
PEER_JB = 64
GELU_C1 = -2.0 * math.sqrt(2.0 / math.pi) * LOG2E
GELU_C2 = GELU_C1 * 0.044715


def _gelu_tanh(x):
    return x * (1.0 / (1.0 + jnp.exp2(x * (GELU_C1 + GELU_C2 * (x * x)))))


def _peer_kernel(hb_ref, u_ref, vt_ref, rk_ref, cnt_ref, e1_ref, e2_ref, h_ref, g_ref, b_ref,
                 o_ref, ob_ref, acc_ref, p_ref):
    e = pl.program_id(1)
    tt = hb_ref.shape[0]

    @pl.when(e == 0)
    def _():
        acc_ref[...] = jnp.zeros_like(acc_ref)

    def act_block(k):
        a = lax.dot_general(u_ref[k * PEER_EB:(k + 1) * PEER_EB, :], hb_ref[...], _NT, preferred_element_type=F32)
        p_ref[k] = _gelu_tanh(a.astype(p_ref.dtype))

    def row_pairs(ref, h, k, ii, ls):
        return pltpu.bitcast(jnp.broadcast_to(ref[h, k, ii:ii + 1, ls], (PEER_JB // 2, LANES)), BF16)

    def gate_block(k):
        for lc in range(tt // LANES):
            ls = slice(lc * LANES, (lc + 1) * LANES)
            for jb in range(PEER_N_KEYS // PEER_JB):
                js = slice(jb * PEER_JB, (jb + 1) * PEER_JB)
                w = [jnp.zeros((PEER_JB, LANES), BF16) for _ in range(PEER_IG)]
                for h in range(PEER_HEADS):
                    cnt = cnt_ref[h, js, ls]
                    e2 = e2_ref[h, js, ls]
                    for ii in range(PEER_IG):
                        gate = row_pairs(e1_ref, h, k, ii, ls) * e2
                        w[ii] = jnp.where(row_pairs(rk_ref, h, k, ii, ls) < cnt, w[ii] + gate, w[ii])
                for ii in range(PEER_IG):
                    rs = slice(ii * PEER_N_KEYS + jb * PEER_JB, ii * PEER_N_KEYS + (jb + 1) * PEER_JB)
                    p_ref[k, rs, ls] = w[ii] * p_ref[k, rs, ls]

    act_block(0)
    for k in range(PEER_NB):
        if k + 1 < PEER_NB:
            act_block(k + 1)
        gate_block(k)
        acc_ref[...] += jnp.dot(vt_ref[:, k * PEER_EB:(k + 1) * PEER_EB], p_ref[k], preferred_element_type=F32)

    @pl.when(e == pl.num_programs(1) - 1)
    def _():
        ff = acc_ref[...].T
        y = _layer_norm(DEEPNORM_ALPHA * h_ref[...] + ff, g_ref[...], b_ref[...])
        o_ref[...] = y
        ob_ref[...] = y.astype(ob_ref.dtype)


def _peer_dense(hb, h, u_b, vt_b, rk, e1, cnt, e2, ln_g, ln_b):
    t = h.shape[0]
    d = D_MODEL
    tt = min(PEER_TT, t)
    ne = PEER_N_EXPERTS // PEER_E
    big = pl.BlockSpec((PEER_HEADS, PEER_N_KEYS, tt), lambda i, e: (0, 0, i))
    grp = pl.BlockSpec((PEER_HEADS, PEER_NB, PEER_IG, tt), lambda i, e: (0, e, 0, i))
    row = pl.BlockSpec((1, d), lambda i, e: (0, 0))
    tok = pl.BlockSpec((tt, d), lambda i, e: (i, 0))
    grouped = lambda a: a.reshape(PEER_HEADS, PEER_N_KEYS // PEER_IG, PEER_IG, t)
    return pl.pallas_call(
        _peer_kernel,
        grid=(t // tt, ne),
        in_specs=[tok, pl.BlockSpec((PEER_E, d), lambda i, e: (e, 0)), pl.BlockSpec((d, PEER_E), lambda i, e: (0, e)),
                  grp, big, grp, big, tok, row, row],
        out_specs=[pl.BlockSpec((tt, d), lambda i, e: (i, 0)), pl.BlockSpec((tt, d), lambda i, e: (i, 0))],
        out_shape=[jax.ShapeDtypeStruct((t, d), F32), jax.ShapeDtypeStruct((t, d), BF16)],
        scratch_shapes=[pltpu.VMEM((d, tt), F32), pltpu.VMEM((PEER_NB, PEER_EB, tt), BF16)],
        compiler_params=_params("parallel", "arbitrary"),
        name="peer_dense",
    )(hb, u_b, vt_b, grouped(rk), cnt, grouped(e1), e2, h, ln_g.reshape(1, d).astype(F32),
      ln_b.reshape(1, d).astype(F32))


def kernel(x, positions, w_in, gate_bias, rel_bias, conv_w, conv_b, lru_w_r, lru_b_r, lru_w_i, lru_b_i, lru_lambda,
           diff_lambda, diff_subln_g, w_branch, w_out, ln1_g, ln1_b, peer_w_q, peer_sub_keys, peer_u, peer_v,
           ln2_g, ln2_b):
    batch, seq, d = x.shape
    t = batch * seq
    h = x.reshape(t, d).astype(F32)
    hb = h.astype(BF16)
    pos_col = positions.reshape(t, 1).astype(jnp.int32)
    inv_freq = jnp.power(ROPE_THETA, -jnp.arange(0, HEAD_DIM, 2, dtype=F32) / HEAD_DIM)
    inv_tiled = jnp.tile(inv_freq, LANES // (HEAD_DIM // 2)).reshape(1, LANES)

    for l in range(DEPTH):
        lam_init = 0.8 - 0.6 * math.exp(-0.3 * l)
        z = _matmul(hb, w_in[l].astype(BF16), BF16)
        ya = _band_attention(z, _band_bias_table(rel_bias[l]), batch, seq)
        yb = _rglru(z, conv_w[l], conv_b[l], lru_w_r[l], lru_b_r[l], lru_w_i[l], lru_b_i[l], lru_lambda[l],
                    batch, seq)
        qr, kr = _rope(z, pos_col, inv_tiled)
        yc = _diff_attention(qr, kr, z, diff_lambda[l], diff_subln_g[l], lam_init, batch, seq)
        h, hb = _merge(ya, yb, yc, z, gate_bias[l], w_branch[l], w_out[l], h, ln1_g[l], ln1_b[l])
        rk, e1, cnt, e2 = _peer_route(hb, peer_w_q[l].T.astype(BF16), peer_sub_keys[l].astype(BF16))
        h, hb = _peer_dense(hb, h, peer_u[l].astype(BF16), peer_v[l].T.astype(BF16), rk, e1, cnt, e2,
                            ln2_g[l], ln2_b[l])
    return h.reshape(batch, seq, d)
```

```python
import functools
import math

import numpy as np
import jax
import jax.numpy as jnp
from jax import lax
from jax.experimental import pallas as pl
from jax.experimental.pallas import tpu as pltpu

D_MODEL = 1024
DEPTH = 2
CHUNK = 64
HEAD_DIM = 64
A_HEADS = 8
A_LOOKBACK = 8
A_MAX_REL = 256
LRU_WIDTH = 512
LRU_BLOCKS = 8
LRU_BLOCK_DIM = 64
CONV_WIDTH = 4
LRU_C = 8.0
C_HEADS = 4
ROPE_THETA = 10000.0
N_BRANCHES = 3
BRANCH_WIDTH = 512
PEER_HEADS = 8
PEER_N_KEYS = 128
PEER_N_EXPERTS = PEER_N_KEYS * PEER_N_KEYS
PEER_HALF = 128
PEER_TOPK = 16
DEEPNORM_ALPHA = (2 * DEPTH) ** 0.25
LN_EPS = 1e-5
NEG_INF = -1e30

LANES = 128
VMEM_LIMIT = 56 * 1024 * 1024

COL_AQ, COL_AK, COL_AV = 0, 512, 1024
COL_BX, COL_BG = 1536, 2048
COL_CQ, COL_CK, COL_CV = 2560, 3072, 3584
COL_GL = 4096
IN_COLS = 7168

F32 = jnp.float32
BF16 = jnp.bfloat16
_NT = (((1,), (1,)), ((), ()))


def _params(*sem):
    return pltpu.CompilerParams(dimension_semantics=sem, vmem_limit_bytes=VMEM_LIMIT)


def _layer_norm(y, g, b):
    mu = jnp.mean(y, axis=-1, keepdims=True)
    d = y - mu
    var = jnp.mean(d * d, axis=-1, keepdims=True)
    return d * lax.rsqrt(var + LN_EPS) * g + b


def _mm_kernel(x_ref, w_ref, o_ref):
    o_ref[...] = jnp.dot(x_ref[...], w_ref[...], preferred_element_type=F32).astype(o_ref.dtype)


def _matmul(x, w, out_dtype, tm=1024, tn=1024):
    m, k = x.shape
    n = w.shape[1]
    return pl.pallas_call(
        _mm_kernel,
        grid=(n // tn, m // tm),
        in_specs=[pl.BlockSpec((tm, k), lambda j, i: (i, 0)),
                  pl.BlockSpec((k, tn), lambda j, i: (0, j))],
        out_specs=pl.BlockSpec((tm, tn), lambda j, i: (i, j)),
        out_shape=jax.ShapeDtypeStruct((m, n), out_dtype),
        compiler_params=_params("parallel", "parallel"),
        name="in_proj",
    )(x, w)


def _rope_kernel(pos_ref, inv_ref, q_ref, k_ref, qo_ref, ko_ref):
    tt = pos_ref.shape[0]
    ang = pos_ref[...].astype(F32) * inv_ref[...]
    c = jnp.cos(ang)
    s = jnp.sin(ang)
    lane = lax.broadcasted_iota(jnp.int32, (tt, LANES), 1)
    first = (lane % HEAD_DIM) < (HEAD_DIM // 2)
    s_signed = jnp.where(first, -s, s)

    def rot(x):
        partner = jnp.where(first, pltpu.roll(x, LANES - HEAD_DIM // 2, 1), pltpu.roll(x, HEAD_DIM // 2, 1))
        return x * c + partner * s_signed

    for blk in range(q_ref.shape[1] // LANES):
        sl = slice(blk * LANES, (blk + 1) * LANES)
        qo_ref[:, sl] = (rot(q_ref[:, sl].astype(F32)) * (LOG2E * HEAD_DIM ** -0.5)).astype(qo_ref.dtype)
        ko_ref[:, sl] = rot(k_ref[:, sl].astype(F32)).astype(ko_ref.dtype)


def _rope(z, pos_col, inv_tiled, tt=512):
    t = z.shape[0]
    w = 512
    return pl.pallas_call(
        _rope_kernel,
        grid=(t // tt,),
        in_specs=[pl.BlockSpec((tt, 1), lambda i: (i, 0)),
                  pl.BlockSpec((1, LANES), lambda i: (0, 0)),
                  pl.BlockSpec((tt, w), lambda i: (i, COL_CQ // w)),
                  pl.BlockSpec((tt, w), lambda i: (i, COL_CK // w))],
        out_specs=[pl.BlockSpec((tt, w), lambda i: (i, 0)),
                   pl.BlockSpec((tt, w), lambda i: (i, 0))],
        out_shape=[jax.ShapeDtypeStruct((t, w), BF16), jax.ShapeDtypeStruct((t, w), BF16)],
        compiler_params=_params("parallel"),
        name="rope",
    )(pos_col, inv_tiled, z, z)


BAND_Q = 2 * CHUNK
BAND_K = (A_LOOKBACK + 2) * CHUNK
BAND_PAD = A_LOOKBACK * CHUNK


def _band_bias_table(rel_bias):
    rb = rel_bias.astype(F32)
    nh, rel = rb.shape
    far = jnp.broadcast_to(rb[:, -1:], (nh, BAND_PAD - A_MAX_REL))
    near = jnp.broadcast_to(rb[:, :1], (nh, BAND_K - (BAND_PAD - A_MAX_REL) - rel))
    wrap = jnp.broadcast_to(rb[:, -1:], (nh, BAND_Q))
    v = jnp.concatenate([far, rb[:, ::-1], near, wrap], axis=1)
    period = BAND_K + BAND_Q
    flat = jnp.tile(v, (1, BAND_Q))[:, :BAND_Q * (period - 1)]
    bias = flat.reshape(nh, BAND_Q, period - 1)[:, :, :BAND_K]
    qi = np.arange(BAND_Q)[:, None]
    kj = np.arange(BAND_K)[None, :]
    q_chunk = qi // CHUNK + A_LOOKBACK
    k_chunk = kj // CHUNK
    allowed = (k_chunk <= q_chunk) & (k_chunk >= q_chunk - A_LOOKBACK)
    return jnp.where(jnp.asarray(allowed)[None], bias, NEG_INF)


def _band_kernel(q_ref, k_ref, v_ref, tab_ref, o_ref, kp_ref, vp_ref):
    c = pl.program_id(2)

    @pl.when(c == 0)
    def _():
        zeros = jnp.zeros((BAND_PAD, LANES), kp_ref.dtype)
        kp_ref[0:BAND_PAD, :] = zeros
        vp_ref[0:BAND_PAD, :] = zeros
        kp_ref[BAND_PAD:, :] = k_ref[...]
        vp_ref[BAND_PAD:, :] = v_ref[...]

    lane = lax.broadcasted_iota(jnp.int32, (BAND_Q, LANES), 1)
    col = lax.broadcasted_iota(jnp.int32, (BAND_Q, BAND_K), 1)
    firsts = [(c * BAND_NBLK + blk) * BAND_Q for blk in range(BAND_NBLK)]
    starts = [pl.multiple_of(f, BAND_Q) for f in firsts]
    kws = [kp_ref[pl.ds(st, BAND_K), :] for st in starts]
    scores = []
    for blk in range(BAND_NBLK):
        q = q_ref[blk * BAND_Q:(blk + 1) * BAND_Q, :]
        for hh in range(2):
            keep = (lane < HEAD_DIM) if hh == 0 else (lane >= HEAD_DIM)
            qm = jnp.where(keep, q, jnp.zeros_like(q))
            scores.append(lax.dot_general(qm, kws[blk], _NT, preferred_element_type=F32))
    probs = []
    for blk in range(BAND_NBLK):
        valid = (col + firsts[blk]) >= BAND_PAD
        for hh in range(2):
            s = scores[2 * blk + hh] * (HEAD_DIM ** -0.5)
            s = jnp.where(valid, s + tab_ref[hh], NEG_INF)
            m = jnp.max(s, axis=-1, keepdims=True)
            e = jnp.exp(s - m)
            probs.append((e / jnp.sum(e, axis=-1, keepdims=True)).astype(kp_ref.dtype))
    for blk in range(BAND_NBLK):
        vw = vp_ref[pl.ds(starts[blk], BAND_K), :]
        outs = [jnp.dot(probs[2 * blk + hh], vw, preferred_element_type=F32) for hh in range(2)]
        o_ref[blk * BAND_Q:(blk + 1) * BAND_Q, :] = jnp.where(lane < HEAD_DIM, outs[0], outs[1]).astype(o_ref.dtype)


BAND_NBLK = 4


def _band_attention(z, table, batch, seq):
    t = z.shape[0]
    nq = seq // (BAND_Q * BAND_NBLK)
    return pl.pallas_call(
        _band_kernel,
        grid=(batch, A_HEADS // 2, nq),
        in_specs=[pl.BlockSpec((BAND_Q * BAND_NBLK, LANES), lambda b, g, c: (b * nq + c, COL_AQ // LANES + g)),
                  pl.BlockSpec((seq, LANES), lambda b, g, c: (b, COL_AK // LANES + g)),
                  pl.BlockSpec((seq, LANES), lambda b, g, c: (b, COL_AV // LANES + g)),
                  pl.BlockSpec((2, BAND_Q, BAND_K), lambda b, g, c: (g, 0, 0))],
        out_specs=pl.BlockSpec((BAND_Q * BAND_NBLK, LANES), lambda b, g, c: (b * nq + c, g)),
        out_shape=jax.ShapeDtypeStruct((t, A_HEADS * HEAD_DIM), BF16),
        scratch_shapes=[pltpu.VMEM((seq + BAND_PAD, LANES), z.dtype),
                        pltpu.VMEM((seq + BAND_PAD, LANES), z.dtype)],
        compiler_params=_params("parallel", "parallel", "arbitrary"),
        name="band_attn",
    )(z, z, z, table)


LRU_ROWS = 8


def _lru_kernel(bx_ref, bg_ref, cw_ref, cb_ref, wr_ref, br_ref, wi_ref, bi_ref, lam_ref, o_ref,
                tail_ref, h_ref, a_ref, u_ref):
    tt = bx_ref.shape[0]

    @pl.when(pl.program_id(1) == 0)
    def _():
        tail_ref[...] = jnp.zeros_like(tail_ref)
        h_ref[...] = jnp.zeros_like(h_ref)

    x = bx_ref[...].astype(F32)
    xx = jnp.concatenate([tail_ref[...], x], axis=0)
    tail_ref[...] = x[tt - 8:, :]
    xc = cb_ref[...] + xx[5:5 + tt, :] * cw_ref[0:1, :]
    for tap in range(1, CONV_WIDTH):
        xc = xc + xx[5 + tap:5 + tap + tt, :] * cw_ref[tap:tap + 1, :]
    xcb = xc.astype(BF16)
    r = jax.nn.sigmoid(jnp.dot(xcb, wr_ref[...], preferred_element_type=F32) + br_ref[...])
    i = jax.nn.sigmoid(jnp.dot(xcb, wi_ref[...], preferred_element_type=F32) + bi_ref[...])
    log_a = LRU_C * r * jax.nn.log_sigmoid(lam_ref[...])
    a = jnp.exp(log_a)
    a_ref[...] = a
    u_ref[...] = jnp.sqrt(1.0 - a * a) * (i * xc)

    def group(gi, h):
        base = pl.multiple_of(gi * LRU_ROWS, LRU_ROWS)
        for rr in range(LRU_ROWS):
            h = a_ref[pl.ds(base + rr, 1), :] * h + u_ref[pl.ds(base + rr, 1), :]
            u_ref[pl.ds(base + rr, 1), :] = h
        return h

    h_ref[...] = lax.fori_loop(0, tt // LRU_ROWS, group, h_ref[...])
    o_ref[...] = (u_ref[...] * jax.nn.gelu(bg_ref[...].astype(F32))).astype(o_ref.dtype)


def _block_diag(w):
    eye = jnp.eye(LRU_BLOCKS, dtype=w.dtype)
    return jnp.einsum('gij,gh->gihj', w, eye).reshape(LRU_WIDTH, LRU_WIDTH)


def _rglru(z, conv_w, conv_b, w_r, b_r, w_i, b_i, lru_lambda, batch, seq, tt=512):
    t = z.shape[0]
    nt = seq // tt
    w = LRU_WIDTH
    row = lambda a: a.reshape(1, w).astype(F32)
    full = lambda shape: pl.BlockSpec(shape, lambda b, j: (0,) * len(shape))
    return pl.pallas_call(
        _lru_kernel,
        grid=(batch, nt),
        in_specs=[pl.BlockSpec((tt, w), lambda b, j: (b * nt + j, COL_BX // w)),
                  pl.BlockSpec((tt, w), lambda b, j: (b * nt + j, COL_BG // w)),
                  full((CONV_WIDTH, w)), full((1, w)), full((w, w)), full((1, w)), full((w, w)), full((1, w)),
                  full((1, w))],
        out_specs=pl.BlockSpec((tt, w), lambda b, j: (b * nt + j, 0)),
        out_shape=jax.ShapeDtypeStruct((t, w), BF16),
        scratch_shapes=[pltpu.VMEM((8, w), F32), pltpu.VMEM((1, w), F32),
                        pltpu.VMEM((tt, w), F32), pltpu.VMEM((tt, w), F32)],
        compiler_params=_params("parallel", "arbitrary"),
        name="rglru",
    )(z, z, conv_w.astype(F32), row(conv_b), _block_diag(w_r).astype(BF16), row(b_r),
      _block_diag(w_i).astype(BF16), row(b_i), row(lru_lambda))


DIFF_TQ = 1024
DIFF_TK = 1024
LOG2E = 1.4426950408889634
DIFF_NSUB = 4


def _diff_kernel(q_ref, k_ref, v_ref, dl_ref, g_ref, o_ref, m_ref, l_ref, acc_ref, *, lam_init):
    qi = pl.program_id(2)
    tq, tk = DIFF_TQ, DIFF_TK
    q = q_ref[...]
    lane = lax.broadcasted_iota(jnp.int32, (tq, LANES), 1)
    qs = (jnp.where(lane < HEAD_DIM, q, jnp.zeros_like(q)), jnp.where(lane >= HEAD_DIM, q, jnp.zeros_like(q)))
    m_ref[...] = jnp.full(m_ref.shape, NEG_INF, F32)
    l_ref[...] = jnp.zeros(l_ref.shape, F32)
    acc_ref[...] = jnp.zeros(acc_ref.shape, F32)

    def block(j, masked):
        start = pl.multiple_of(j * tk, tk)
        kb = k_ref[pl.ds(start, tk), :]
        vtb = v_ref[j]
        chains = [(sub, mp) for sub in range(DIFF_NSUB) for mp in range(2)]
        tsub = tq // DIFF_NSUB
        scores = [lax.dot_general(kb, qs[mp][sub * tsub:(sub + 1) * tsub], _NT, preferred_element_type=F32)
                  for sub, mp in chains]
        for (sub, mp), s in zip(chains, scores):
            cols = slice(sub * tsub, (sub + 1) * tsub)
            if masked:
                key = lax.broadcasted_iota(jnp.int32, (tk, tsub), 0)
                qry = lax.broadcasted_iota(jnp.int32, (tk, tsub), 1) + sub * tsub
                s = jnp.where((key // CHUNK) <= (qry // CHUNK), s, NEG_INF)
            m_old = m_ref[mp, :, cols]
            m_new = jnp.maximum(m_old, jnp.max(s, axis=0, keepdims=True))
            alpha = jnp.exp2(m_old - m_new)
            p = jnp.exp2(s - m_new)
            l_ref[mp, :, cols] = alpha * l_ref[mp, :, cols] + jnp.sum(p, axis=0, keepdims=True)
            acc_ref[mp, :, cols] = alpha * acc_ref[mp, :, cols] + jnp.dot(vtb, p.astype(vtb.dtype),
                                                                          preferred_element_type=F32)
            m_ref[mp, :, cols] = m_new

    def body(j, carry):
        block(j, False)
        return carry

    lax.fori_loop(0, qi, body, 0)
    block(qi, True)

    dl = dl_ref[...]
    lam = (jnp.exp(jnp.sum(dl[0:1] * dl[1:2], axis=-1, keepdims=True))
           - jnp.exp(jnp.sum(dl[2:3] * dl[3:4], axis=-1, keepdims=True)) + lam_init)
    o = acc_ref[0] / l_ref[0] - lam * (acc_ref[1] / l_ref[1])
    ms = jnp.mean(o * o, axis=0, keepdims=True)
    o = o * lax.rsqrt(ms + LN_EPS) * g_ref[...] * (1.0 - lam_init)
    o_ref[...] = o.T.astype(o_ref.dtype)


def _diff_attention(qr, kr, z, diff_lambda, subln_g, lam_init, batch, seq):
    t = z.shape[0]
    nq = seq // DIFF_TQ
    nk = seq // DIFF_TK
    hd = 2 * HEAD_DIM
    vt = z[:, COL_CV:COL_CV + C_HEADS * hd].reshape(t // DIFF_TK, DIFF_TK, C_HEADS * hd).transpose(0, 2, 1)
    return pl.pallas_call(
        functools.partial(_diff_kernel, lam_init=lam_init),
        grid=(batch, C_HEADS, nq),
        in_specs=[pl.BlockSpec((DIFF_TQ, LANES), lambda b, h, i: (b * nq + i, h)),
                  pl.BlockSpec((seq, LANES), lambda b, h, i: (b, h)),
                  pl.BlockSpec((nk, hd, DIFF_TK), lambda b, h, i: (b, h, 0)),
                  pl.BlockSpec((4, HEAD_DIM), lambda b, h, i: (0, 0)),
                  pl.BlockSpec((hd, 1), lambda b, h, i: (0, 0))],
        out_specs=pl.BlockSpec((DIFF_TQ, LANES), lambda b, h, i: (b * nq + i, h)),
        out_shape=jax.ShapeDtypeStruct((t, C_HEADS * hd), BF16),
        scratch_shapes=[pltpu.VMEM((2, 1, DIFF_TQ), F32), pltpu.VMEM((2, 1, DIFF_TQ), F32),
                        pltpu.VMEM((2, hd, DIFF_TQ), F32)],
        compiler_params=_params("parallel", "parallel", "arbitrary"),
        name="diff_attn",
    )(qr, kr, vt, diff_lambda.astype(F32), subln_g.reshape(hd, 1).astype(F32))


def _merge_kernel(ya_ref, yb_ref, yc_ref, gl0_ref, gl1_ref, gl2_ref, gb_ref, wb_ref, wo_ref, h_ref, g_ref, b_ref,
                  o_ref, ob_ref):
    merged = None
    for n, (y_ref, gl_ref) in enumerate(((ya_ref, gl0_ref), (yb_ref, gl1_ref), (yc_ref, gl2_ref))):
        proj = jnp.dot(y_ref[...], wb_ref[n], preferred_element_type=F32)
        gate = jax.nn.sigmoid(gl_ref[...].astype(F32) + gb_ref[n:n + 1, :])
        merged = gate * proj if merged is None else merged + gate * proj
    mix = jnp.dot(merged.astype(BF16), wo_ref[...], preferred_element_type=F32)
    y = _layer_norm(DEEPNORM_ALPHA * h_ref[...] + mix, g_ref[...], b_ref[...])
    o_ref[...] = y
    ob_ref[...] = y.astype(ob_ref.dtype)


def _merge(ya, yb, yc, z, gate_bias, w_branch, w_out, h, ln_g, ln_b, tt=256):
    t = h.shape[0]
    d = D_MODEL
    bw = BRANCH_WIDTH
    ytile = pl.BlockSpec((tt, bw), lambda i: (i, 0))
    gl = lambda n: pl.BlockSpec((tt, d), lambda i: (i, COL_GL // d + n))
    full = lambda shape: pl.BlockSpec(shape, lambda i: (0,) * len(shape))
    return pl.pallas_call(
        _merge_kernel,
        grid=(t // tt,),
        in_specs=[ytile, ytile, ytile, gl(0), gl(1), gl(2), full((N_BRANCHES, d)), full((N_BRANCHES, bw, d)),
                  full((d, d)), pl.BlockSpec((tt, d), lambda i: (i, 0)), full((1, d)), full((1, d))],
        out_specs=[pl.BlockSpec((tt, d), lambda i: (i, 0)), pl.BlockSpec((tt, d), lambda i: (i, 0))],
        out_shape=[jax.ShapeDtypeStruct((t, d), F32), jax.ShapeDtypeStruct((t, d), BF16)],
        compiler_params=_params("parallel"),
        name="merge_out_ln",
    )(ya, yb, yc, z, z, z, gate_bias.astype(F32), w_branch.astype(BF16), w_out.astype(BF16), h,
      ln_g.reshape(1, d).astype(F32), ln_b.reshape(1, d).astype(F32))


def _top_rows(s, k, with_rank=False):
    rows = []
    cur = s
    rank = jnp.full(s.shape, float(k), F32) if with_rank else None
    for r in range(k):
        m = jnp.max(cur, axis=0, keepdims=True)
        rows.append(m)
        hit = cur == m
        if with_rank:
            rank = jnp.where(hit, float(r), rank)
        cur = jnp.where(hit, -jnp.inf, cur)
    return (rows, rank) if with_rank else rows


def _batcher_pairs(n):
    pairs = []
    p = 1
    while p < n:
        k = p
        while k >= 1:
            for j in range(k % p, n - k, 2 * k):
                for i in range(min(k, n - j - k)):
                    if (i + j) // (2 * p) == (i + j + k) // (2 * p):
                        pairs.append((i + j, i + j + k))
            k //= 2
        p *= 2
    return pairs


def _sorted_top(s, k):
    sub = s.shape[0] // k
    x = [s[i * sub:(i + 1) * sub, :] for i in range(k)]

    def exchange(lo, hi):
        x[lo], x[hi] = jnp.maximum(x[lo], x[hi]), jnp.minimum(x[lo], x[hi])

    for lo, hi in _batcher_pairs(k):
        exchange(lo, hi)
    shift = sub // 2
    while shift >= 1:
        y = [pltpu.roll(v, shift, 0) for v in x]
        x = [jnp.maximum(x[i], y[k - 1 - i]) for i in range(k)]
        stride = k // 2
        while stride >= 1:
            for i in range(k):
                if i & stride == 0:
                    exchange(i, i + stride)
            stride //= 2
        shift //= 2
    return x


def _count_prefix(a, pred):
    c8 = pred(a[7])
    c4 = pred(jnp.where(c8, a[11], a[3]))
    c2 = pred(jnp.where(c8, jnp.where(c4, a[13], a[9]), jnp.where(c4, a[5], a[1])))
    hi = jnp.where(c4, jnp.where(c2, a[14], a[12]), jnp.where(c2, a[10], a[8]))
    lo = jnp.where(c4, jnp.where(c2, a[6], a[4]), jnp.where(c2, a[2], a[0]))
    c1 = pred(jnp.where(c8, hi, lo))
    c0 = pred(a[15])
    one = lambda c, v: jnp.where(c, v, 0.0)
    return one(c8, 8.0) + one(c4, 4.0) + one(c2, 2.0) + one(c1, 1.0) + one(c0, 1.0)


def _pair_word(x):
    bits = lax.bitcast_convert_type(x.astype(BF16).astype(F32), jnp.uint32)
    return bits | (bits >> 16)


def _route_kernel(h_ref, wq_ref, keys_ref, rk_ref, e1_ref, cnt_ref, e2_ref):
    qt = lax.dot_general(wq_ref[...], h_ref[...], _NT, preferred_element_type=F32).astype(BF16)
    for h in range(PEER_HEADS):
        sc = []
        for p in range(2):
            r0 = (h * 2 + p) * PEER_HALF
            sc.append(jnp.dot(keys_ref[h, p], qt[r0:r0 + PEER_HALF, :], preferred_element_type=F32))
        s1, s2 = sc
        a8 = _sorted_top(s1, PEER_TOPK)
        b8 = _sorted_top(s2, PEER_TOPK)
        a = [v[0:1, :] for v in a8]
        b = [v[0:1, :] for v in b8]
        cand = jnp.concatenate([a[k] + b[l] for k in range(PEER_TOPK) for l in range(PEER_TOPK // (k + 1))], axis=0)
        tau = _top_rows(cand, PEER_TOPK)[PEER_TOPK - 1]
        top = a[0] + b[0]
        zsum = jnp.sum(jnp.where(cand >= tau, jnp.exp(cand - top), 0.0), axis=0, keepdims=True)
        tau8 = jnp.broadcast_to(tau, a8[0].shape)
        sub = a8[0].shape[0]
        slabs = lambda s: [s[i * sub:(i + 1) * sub, :] for i in range(s.shape[0] // sub)]
        rank1 = jnp.concatenate([_count_prefix(a8, lambda v: v > x) for x in slabs(s1)], axis=0)
        cnt = jnp.concatenate([_count_prefix(a8, lambda v: v + x >= tau8) for x in slabs(s2)], axis=0)
        rk_ref[h] = _pair_word(rank1)
        e1_ref[h] = _pair_word(jnp.exp(s1 - a[0]) / zsum)
        cnt_ref[h] = cnt.astype(cnt_ref.dtype)
        e2_ref[h] = jnp.exp(s2 - b[0]).astype(e2_ref.dtype)


def _peer_route(hb, wq_t, keys, tt=256):
    t = hb.shape[0]
    shape = (PEER_HEADS, PEER_N_KEYS, t)
    bspec = pl.BlockSpec((PEER_HEADS, PEER_N_KEYS, tt), lambda i: (0, 0, i))
    return pl.pallas_call(
        _route_kernel,
        grid=(t // tt,),
        in_specs=[pl.BlockSpec((tt, D_MODEL), lambda i: (i, 0)),
                  pl.BlockSpec(wq_t.shape, lambda i: (0, 0)),
                  pl.BlockSpec(keys.shape, lambda i: (0, 0, 0, 0))],
        out_specs=[bspec, bspec, bspec, bspec],
        out_shape=[jax.ShapeDtypeStruct(shape, jnp.uint32), jax.ShapeDtypeStruct(shape, jnp.uint32),
                   jax.ShapeDtypeStruct(shape, BF16), jax.ShapeDtypeStruct(shape, BF16)],
        compiler_params=_params("parallel"),
        name="peer_route",
    )(hb, wq_t, keys)


PEER_TT = 512
PEER_IG = 8
PEER_EB = PEER_IG * PEER_N_KEYS
PEER_NB = 2
PEER_E = PEER_NB * PEER_EB
PEER_JB = 64
GELU_C1 = -2.0 * math.sqrt(2.0 / math.pi) * LOG2E
GELU_C2 = GELU_C1 * 0.044715


def _gelu_tanh(x):
    return x * (1.0 / (1.0 + jnp.exp2(x * (GELU_C1 + GELU_C2 * (x * x)))))


def _peer_kernel(hb_ref, u_ref, vt_ref, rk_ref, cnt_ref, e1_ref, e2_ref, h_ref, g_ref, b_ref,
                 o_ref, ob_ref, acc_ref, p_ref):
    e = pl.program_id(1)
    tt = hb_ref.shape[0]

    @pl.when(e == 0)
    def _():
        acc_ref[...] = jnp.zeros_like(acc_ref)

    def act_rows(k, r0, r1):
        a = lax.dot_general(u_ref[k * PEER_EB + r0:k * PEER_EB + r1, :], hb_ref[...], _NT,
                            preferred_element_type=F32)
        p_ref[k, r0:r1, :] = _gelu_tanh(a.astype(p_ref.dtype))

    def row_pairs(ref, h, k, ii, ls):
        return pltpu.bitcast(jnp.broadcast_to(ref[h, k, ii:ii + 1, ls], (PEER_JB // 2, LANES)), BF16)

    def gate_cols(k, c0, c1):
        for lc in range(c0 // LANES, c1 // LANES):
            ls = slice(lc * LANES, (lc + 1) * LANES)
            for jb in range(PEER_N_KEYS // PEER_JB):
                js = slice(jb * PEER_JB, (jb + 1) * PEER_JB)
                w = [jnp.zeros((PEER_JB, LANES), BF16) for _ in range(PEER_IG)]
                for h in range(PEER_HEADS):
                    cnt = cnt_ref[h, js, ls]
                    e2 = e2_ref[h, js, ls]
                    for ii in range(PEER_IG):
                        gate = row_pairs(e1_ref, h, k, ii, ls) * e2
                        w[ii] = jnp.where(row_pairs(rk_ref, h, k, ii, ls) < cnt, w[ii] + gate, w[ii])
                for ii in range(PEER_IG):
                    rs = slice(ii * PEER_N_KEYS + jb * PEER_JB, ii * PEER_N_KEYS + (jb + 1) * PEER_JB)
                    p_ref[k, rs, ls] = w[ii] * p_ref[k, rs, ls]

    def out_cols(k, c0, c1):
        acc_ref[:, c0:c1] += jnp.dot(vt_ref[:, k * PEER_EB:(k + 1) * PEER_EB], p_ref[k, :, c0:c1],
                                     preferred_element_type=F32)

    act_rows(0, 0, PEER_EB)
    for k in range(PEER_NB):
        if k + 1 < PEER_NB:
            act_rows(k + 1, 0, PEER_EB)
        gate_cols(k, 0, tt)
        out_cols(k, 0, tt)

    @pl.when(e == pl.num_programs(1) - 1)
    def _():
        ff = acc_ref[...].T
        y = _layer_norm(DEEPNORM_ALPHA * h_ref[...] + ff, g_ref[...], b_ref[...])
        o_ref[...] = y
        ob_ref[...] = y.astype(ob_ref.dtype)


def _peer_dense(hb, h, u_b, vt_b, rk, e1, cnt, e2, ln_g, ln_b):
    t = h.shape[0]
    d = D_MODEL
    tt = min(PEER_TT, t)
    ne = PEER_N_EXPERTS // PEER_E
    big = pl.BlockSpec((PEER_HEADS, PEER_N_KEYS, tt), lambda i, e: (0, 0, i))
    grp = pl.BlockSpec((PEER_HEADS, PEER_NB, PEER_IG, tt), lambda i, e: (0, e, 0, i))
    row = pl.BlockSpec((1, d), lambda i, e: (0, 0))
    tok = pl.BlockSpec((tt, d), lambda i, e: (i, 0))
    grouped = lambda a: a.reshape(PEER_HEADS, PEER_N_KEYS // PEER_IG, PEER_IG, t)
    return pl.pallas_call(
        _peer_kernel,
        grid=(t // tt, ne),
        in_specs=[tok, pl.BlockSpec((PEER_E, d), lambda i, e: (e, 0)), pl.BlockSpec((d, PEER_E), lambda i, e: (0, e)),
                  grp, big, grp, big, tok, row, row],
        out_specs=[pl.BlockSpec((tt, d), lambda i, e: (i, 0)), pl.BlockSpec((tt, d), lambda i, e: (i, 0))],
        out_shape=[jax.ShapeDtypeStruct((t, d), F32), jax.ShapeDtypeStruct((t, d), BF16)],
        scratch_shapes=[pltpu.VMEM((d, tt), F32), pltpu.VMEM((PEER_NB, PEER_EB, tt), BF16)],
        compiler_params=_params("parallel", "arbitrary"),
        name="peer_dense",
    )(hb, u_b, vt_b, grouped(rk), cnt, grouped(e1), e2, h, ln_g.reshape(1, d).astype(F32),
      ln_b.reshape(1, d).astype(F32))


def kernel(x, positions, w_in, gate_bias, rel_bias, conv_w, conv_b, lru_w_r, lru_b_r, lru_w_i, lru_b_i, lru_lambda,
           diff_lambda, diff_subln_g, w_branch, w_out, ln1_g, ln1_b, peer_w_q, peer_sub_keys, peer_u, peer_v,
           ln2_g, ln2_b):
    batch, seq, d = x.shape
    t = batch * seq
    h = x.reshape(t, d).astype(F32)
    hb = h.astype(BF16)
    pos_col = positions.reshape(t, 1).astype(jnp.int32)
    inv_freq = jnp.power(ROPE_THETA, -jnp.arange(0, HEAD_DIM, 2, dtype=F32) / HEAD_DIM)
    inv_tiled = jnp.tile(inv_freq, LANES // (HEAD_DIM // 2)).reshape(1, LANES)

    for l in range(DEPTH):
        lam_init = 0.8 - 0.6 * math.exp(-0.3 * l)
        z = _matmul(hb, w_in[l].astype(BF16), BF16)
        ya = _band_attention(z, _band_bias_table(rel_bias[l]), batch, seq)
        yb = _rglru(z, conv_w[l], conv_b[l], lru_w_r[l], lru_b_r[l], lru_w_i[l], lru_b_i[l], lru_lambda[l],
                    batch, seq)
        qr, kr = _rope(z, pos_col, inv_tiled)
        yc = _diff_attention(qr, kr, z, diff_lambda[l], diff_subln_g[l], lam_init, batch, seq)
        h, hb = _merge(ya, yb, yc, z, gate_bias[l], w_branch[l], w_out[l], h, ln1_g[l], ln1_b[l])
        rk, e1, cnt, e2 = _peer_route(hb, peer_w_q[l].T.astype(BF16), peer_sub_keys[l].astype(BF16))
        h, hb = _peer_dense(hb, h, peer_u[l].astype(BF16), peer_v[l].astype(BF16).T, rk, e1, cnt, e2,
                            ln2_g[l], ln2_b[l])
    return h.reshape(batch, seq, d)
```

```python
import functools
import math

import numpy as np
import jax
import jax.numpy as jnp
from jax import lax
from jax.experimental import pallas as pl
from jax.experimental.pallas import tpu as pltpu

D_MODEL = 1024
DEPTH = 2
CHUNK = 64
HEAD_DIM = 64
A_HEADS = 8
A_LOOKBACK = 8
A_MAX_REL = 256
LRU_WIDTH = 512
LRU_BLOCKS = 8
LRU_BLOCK_DIM = 64
CONV_WIDTH = 4
LRU_C = 8.0
C_HEADS = 4
ROPE_THETA = 10000.0
N_BRANCHES = 3
BRANCH_WIDTH = 512
PEER_HEADS = 8
PEER_N_KEYS = 128
PEER_N_EXPERTS = PEER_N_KEYS * PEER_N_KEYS
PEER_HALF = 128
PEER_TOPK = 16
DEEPNORM_ALPHA = (2 * DEPTH) ** 0.25
LN_EPS = 1e-5
NEG_INF = -1e30

LANES = 128
VMEM_LIMIT = 56 * 1024 * 1024

COL_AQ, COL_AK, COL_AV = 0, 512, 1024
COL_BX, COL_BG = 1536, 2048
COL_CQ, COL_CK, COL_CV = 2560, 3072, 3584
COL_GL = 4096
IN_COLS = 7168

F32 = jnp.float32
BF16 = jnp.bfloat16
_NT = (((1,), (1,)), ((), ()))


def _params(*sem):
    return pltpu.CompilerParams(dimension_semantics=sem, vmem_limit_bytes=VMEM_LIMIT)


def _layer_norm(y, g, b):
    mu = jnp.mean(y, axis=-1, keepdims=True)
    d = y - mu
    var = jnp.mean(d * d, axis=-1, keepdims=True)
    return d * lax.rsqrt(var + LN_EPS) * g + b


def _mm_kernel(x_ref, w_ref, o_ref):
    o_ref[...] = jnp.dot(x_ref[...], w_ref[...], preferred_element_type=F32).astype(o_ref.dtype)


def _matmul(x, w, out_dtype, tm=1024, tn=1792):
    m, k = x.shape
    n = w.shape[1]
    return pl.pallas_call(
        _mm_kernel,
        grid=(n // tn, m // tm),
        in_specs=[pl.BlockSpec((tm, k), lambda j, i: (i, 0)),
                  pl.BlockSpec((k, tn), lambda j, i: (0, j))],
        out_specs=pl.BlockSpec((tm, tn), lambda j, i: (i, j)),
        out_shape=jax.ShapeDtypeStruct((m, n), out_dtype),
        compiler_params=_params("parallel", "parallel"),
        name="in_proj",
    )(x, w)


def _rope_kernel(pos_ref, inv_ref, q_ref, k_ref, qo_ref, ko_ref):
    tt = pos_ref.shape[0]
    ang = pos_ref[...].astype(F32) * inv_ref[...]
    c = jnp.cos(ang)
    s = jnp.sin(ang)
    lane = lax.broadcasted_iota(jnp.int32, (tt, LANES), 1)
    first = (lane % HEAD_DIM) < (HEAD_DIM // 2)
    s_signed = jnp.where(first, -s, s)

    def rot(x):
        partner = jnp.where(first, pltpu.roll(x, LANES - HEAD_DIM // 2, 1), pltpu.roll(x, HEAD_DIM // 2, 1))
        return x * c + partner * s_signed

    for blk in range(q_ref.shape[1] // LANES):
        sl = slice(blk * LANES, (blk + 1) * LANES)
        qo_ref[:, sl] = (rot(q_ref[:, sl].astype(F32)) * (LOG2E * HEAD_DIM ** -0.5)).astype(qo_ref.dtype)
        ko_ref[:, sl] = rot(k_ref[:, sl].astype(F32)).astype(ko_ref.dtype)


def _rope(z, pos_col, inv_tiled, tt=512):
    t = z.shape[0]
    w = 512
    return pl.pallas_call(
        _rope_kernel,
        grid=(t // tt,),
        in_specs=[pl.BlockSpec((tt, 1), lambda i: (i, 0)),
                  pl.BlockSpec((1, LANES), lambda i: (0, 0)),
                  pl.BlockSpec((tt, w), lambda i: (i, COL_CQ // w)),
                  pl.BlockSpec((tt, w), lambda i: (i, COL_CK // w))],
        out_specs=[pl.BlockSpec((tt, w), lambda i: (i, 0)),
                   pl.BlockSpec((tt, w), lambda i: (i, 0))],
        out_shape=[jax.ShapeDtypeStruct((t, w), BF16), jax.ShapeDtypeStruct((t, w), BF16)],
        compiler_params=_params("parallel"),
        name="rope",
    )(pos_col, inv_tiled, z, z)


BAND_Q = 2 * CHUNK
BAND_K = (A_LOOKBACK + 2) * CHUNK
BAND_PAD = A_LOOKBACK * CHUNK


def _band_bias_table(rel_bias):
    rb = rel_bias.astype(F32)
    nh, rel = rb.shape
    far = jnp.broadcast_to(rb[:, -1:], (nh, BAND_PAD - A_MAX_REL))
    near = jnp.broadcast_to(rb[:, :1], (nh, BAND_K - (BAND_PAD - A_MAX_REL) - rel))
    wrap = jnp.broadcast_to(rb[:, -1:], (nh, BAND_Q))
    v = jnp.concatenate([far, rb[:, ::-1], near, wrap], axis=1)
    period = BAND_K + BAND_Q
    flat = jnp.tile(v, (1, BAND_Q))[:, :BAND_Q * (period - 1)]
    bias = flat.reshape(nh, BAND_Q, period - 1)[:, :, :BAND_K]
    qi = np.arange(BAND_Q)[:, None]
    kj = np.arange(BAND_K)[None, :]
    q_chunk = qi // CHUNK + A_LOOKBACK
    k_chunk = kj // CHUNK
    allowed = (k_chunk <= q_chunk) & (k_chunk >= q_chunk - A_LOOKBACK)
    return jnp.where(jnp.asarray(allowed)[None], bias, NEG_INF)


def _band_kernel(q_ref, k_ref, v_ref, tab_ref, o_ref, kp_ref, vp_ref):
    c = pl.program_id(2)

    @pl.when(c == 0)
    def _():
        zeros = jnp.zeros((BAND_PAD, LANES), kp_ref.dtype)
        kp_ref[0:BAND_PAD, :] = zeros
        vp_ref[0:BAND_PAD, :] = zeros
        kp_ref[BAND_PAD:, :] = k_ref[...]
        vp_ref[BAND_PAD:, :] = v_ref[...]

    lane = lax.broadcasted_iota(jnp.int32, (BAND_Q, LANES), 1)
    col = lax.broadcasted_iota(jnp.int32, (BAND_Q, BAND_K), 1)
    firsts = [(c * BAND_NBLK + blk) * BAND_Q for blk in range(BAND_NBLK)]
    starts = [pl.multiple_of(f, BAND_Q) for f in firsts]
    kws = [kp_ref[pl.ds(st, BAND_K), :] for st in starts]
    scores = []
    for blk in range(BAND_NBLK):
        q = q_ref[blk * BAND_Q:(blk + 1) * BAND_Q, :]
        for hh in range(2):
            keep = (lane < HEAD_DIM) if hh == 0 else (lane >= HEAD_DIM)
            qm = jnp.where(keep, q, jnp.zeros_like(q))
            scores.append(lax.dot_general(qm, kws[blk], _NT, preferred_element_type=F32))
    probs = []
    for blk in range(BAND_NBLK):
        valid = (col + firsts[blk]) >= BAND_PAD
        for hh in range(2):
            s = scores[2 * blk + hh] * (HEAD_DIM ** -0.5)
            s = jnp.where(valid, s + tab_ref[hh], NEG_INF)
            m = jnp.max(s, axis=-1, keepdims=True)
            e = jnp.exp(s - m)
            probs.append((e / jnp.sum(e, axis=-1, keepdims=True)).astype(kp_ref.dtype))
    for blk in range(BAND_NBLK):
        vw = vp_ref[pl.ds(starts[blk], BAND_K), :]
        outs = [jnp.dot(probs[2 * blk + hh], vw, preferred_element_type=F32) for hh in range(2)]
        o_ref[blk * BAND_Q:(blk + 1) * BAND_Q, :] = jnp.where(lane < HEAD_DIM, outs[0], outs[1]).astype(o_ref.dtype)


BAND_NBLK = 8


def _band_attention(z, table, batch, seq):
    t = z.shape[0]
    nq = seq // (BAND_Q * BAND_NBLK)
    return pl.pallas_call(
        _band_kernel,
        grid=(batch, A_HEADS // 2, nq),
        in_specs=[pl.BlockSpec((BAND_Q * BAND_NBLK, LANES), lambda b, g, c: (b * nq + c, COL_AQ // LANES + g)),
                  pl.BlockSpec((seq, LANES), lambda b, g, c: (b, COL_AK // LANES + g)),
                  pl.BlockSpec((seq, LANES), lambda b, g, c: (b, COL_AV // LANES + g)),
                  pl.BlockSpec((2, BAND_Q, BAND_K), lambda b, g, c: (g, 0, 0))],
        out_specs=pl.BlockSpec((BAND_Q * BAND_NBLK, LANES), lambda b, g, c: (b * nq + c, g)),
        out_shape=jax.ShapeDtypeStruct((t, A_HEADS * HEAD_DIM), BF16),
        scratch_shapes=[pltpu.VMEM((seq + BAND_PAD, LANES), z.dtype),
                        pltpu.VMEM((seq + BAND_PAD, LANES), z.dtype)],
        compiler_params=_params("parallel", "parallel", "arbitrary"),
        name="band_attn",
    )(z, z, z, table)


LRU_ROWS = 8


def _lru_kernel(bx_ref, bg_ref, cw_ref, cb_ref, wr_ref, br_ref, wi_ref, bi_ref, lam_ref, o_ref,
                tail_ref, h_ref, a_ref, u_ref):
    tt = bx_ref.shape[0]

    @pl.when(pl.program_id(1) == 0)
    def _():
        tail_ref[...] = jnp.zeros_like(tail_ref)
        h_ref[...] = jnp.zeros_like(h_ref)

    x = bx_ref[...].astype(F32)
    xx = jnp.concatenate([tail_ref[...], x], axis=0)
    tail_ref[...] = x[tt - 8:, :]
    xc = cb_ref[...] + xx[5:5 + tt, :] * cw_ref[0:1, :]
    for tap in range(1, CONV_WIDTH):
        xc = xc + xx[5 + tap:5 + tap + tt, :] * cw_ref[tap:tap + 1, :]
    xcb = xc.astype(BF16)
    r = jax.nn.sigmoid(jnp.dot(xcb, wr_ref[...], preferred_element_type=F32) + br_ref[...])
    i = jax.nn.sigmoid(jnp.dot(xcb, wi_ref[...], preferred_element_type=F32) + bi_ref[...])
    log_a = LRU_C * r * jax.nn.log_sigmoid(lam_ref[...])
    a = jnp.exp(log_a)
    a_ref[...] = a
    u_ref[...] = jnp.sqrt(1.0 - a * a) * (i * xc)

    def group(gi, h):
        base = pl.multiple_of(gi * LRU_ROWS, LRU_ROWS)
        for rr in range(LRU_ROWS):
            h = a_ref[pl.ds(base + rr, 1), :] * h + u_ref[pl.ds(base + rr, 1), :]
            u_ref[pl.ds(base + rr, 1), :] = h
        return h

    h_ref[...] = lax.fori_loop(0, tt // LRU_ROWS, group, h_ref[...])
    o_ref[...] = (u_ref[...] * jax.nn.gelu(bg_ref[...].astype(F32))).astype(o_ref.dtype)


def _block_diag(w):
    eye = jnp.eye(LRU_BLOCKS, dtype=w.dtype)
    return jnp.einsum('gij,gh->gihj', w, eye).reshape(LRU_WIDTH, LRU_WIDTH)


def _rglru(z, conv_w, conv_b, w_r, b_r, w_i, b_i, lru_lambda, batch, seq, tt=512):
    t = z.shape[0]
    nt = seq // tt
    w = LRU_WIDTH
    row = lambda a: a.reshape(1, w).astype(F32)
    full = lambda shape: pl.BlockSpec(shape, lambda b, j: (0,) * len(shape))
    return pl.pallas_call(
        _lru_kernel,
        grid=(batch, nt),
        in_specs=[pl.BlockSpec((tt, w), lambda b, j: (b * nt + j, COL_BX // w)),
                  pl.BlockSpec((tt, w), lambda b, j: (b * nt + j, COL_BG // w)),
                  full((CONV_WIDTH, w)), full((1, w)), full((w, w)), full((1, w)), full((w, w)), full((1, w)),
                  full((1, w))],
        out_specs=pl.BlockSpec((tt, w), lambda b, j: (b * nt + j, 0)),
        out_shape=jax.ShapeDtypeStruct((t, w), BF16),
        scratch_shapes=[pltpu.VMEM((8, w), F32), pltpu.VMEM((1, w), F32),
                        pltpu.VMEM((tt, w), F32), pltpu.VMEM((tt, w), F32)],
        compiler_params=_params("parallel", "arbitrary"),
        name="rglru",
    )(z, z, conv_w.astype(F32), row(conv_b), _block_diag(w_r).astype(BF16), row(b_r),
      _block_diag(w_i).astype(BF16), row(b_i), row(lru_lambda))


DIFF_TQ = 1024
DIFF_TK = 1024
LOG2E = 1.4426950408889634
DIFF_NSUB = 4


def _diff_kernel(q_ref, k_ref, v_ref, dl_ref, g_ref, o_ref, m_ref, l_ref, acc_ref, *, lam_init):
    qi = pl.program_id(2)
    tq, tk = DIFF_TQ, DIFF_TK
    q = q_ref[...]
    lane = lax.broadcasted_iota(jnp.int32, (tq, LANES), 1)
    qs = (jnp.where(lane < HEAD_DIM, q, jnp.zeros_like(q)), jnp.where(lane >= HEAD_DIM, q, jnp.zeros_like(q)))
    m_ref[...] = jnp.full(m_ref.shape, NEG_INF, F32)
    l_ref[...] = jnp.zeros(l_ref.shape, F32)
    acc_ref[...] = jnp.zeros(acc_ref.shape, F32)

    def block(j, masked):
        start = pl.multiple_of(j * tk, tk)
        kb = k_ref[pl.ds(start, tk), :]
        vtb = v_ref[j]
        chains = [(sub, mp) for sub in range(DIFF_NSUB) for mp in range(2)]
        tsub = tq // DIFF_NSUB
        scores = [lax.dot_general(kb, qs[mp][sub * tsub:(sub + 1) * tsub], _NT, preferred_element_type=F32)
                  for sub, mp in chains]
        weights = []
        for (sub, mp), s in zip(chains, scores):
            cols = slice(sub * tsub, (sub + 1) * tsub)
            if masked:
                key = lax.broadcasted_iota(jnp.int32, (tk, tsub), 0)
                qry = lax.broadcasted_iota(jnp.int32, (tk, tsub), 1) + sub * tsub
                s = jnp.where((key // CHUNK) <= (qry // CHUNK), s, NEG_INF)
            m_old = m_ref[mp, :, cols]
            m_new = jnp.maximum(m_old, jnp.max(s, axis=0, keepdims=True))
            alpha = jnp.exp2(m_old - m_new)
            p = jnp.exp2(s - m_new)
            l_ref[mp, :, cols] = alpha * l_ref[mp, :, cols] + jnp.sum(p, axis=0, keepdims=True)
            m_ref[mp, :, cols] = m_new
            weights.append((alpha, p.astype(vtb.dtype)))
        for (sub, mp), (alpha, p) in zip(chains, weights):
            cols = slice(sub * tsub, (sub + 1) * tsub)
            acc_ref[mp, :, cols] = alpha * acc_ref[mp, :, cols] + jnp.dot(vtb, p, preferred_element_type=F32)

    def body(j, carry):
        block(j, False)
        return carry

    lax.fori_loop(0, qi, body, 0)
    block(qi, True)

    dl = dl_ref[...]
    lam = (jnp.exp(jnp.sum(dl[0:1] * dl[1:2], axis=-1, keepdims=True))
           - jnp.exp(jnp.sum(dl[2:3] * dl[3:4], axis=-1, keepdims=True)) + lam_init)
    o = acc_ref[0] / l_ref[0] - lam * (acc_ref[1] / l_ref[1])
    ms = jnp.mean(o * o, axis=0, keepdims=True)
    o = o * lax.rsqrt(ms + LN_EPS) * g_ref[...] * (1.0 - lam_init)
    o_ref[...] = o.T.astype(o_ref.dtype)


def _diff_attention(qr, kr, z, diff_lambda, subln_g, lam_init, batch, seq):
    t = z.shape[0]
    nq = seq // DIFF_TQ
    nk = seq // DIFF_TK
    hd = 2 * HEAD_DIM
    vt = z[:, COL_CV:COL_CV + C_HEADS * hd].reshape(t // DIFF_TK, DIFF_TK, C_HEADS * hd).transpose(0, 2, 1)
    return pl.pallas_call(
        functools.partial(_diff_kernel, lam_init=lam_init),
        grid=(batch, C_HEADS, nq),
        in_specs=[pl.BlockSpec((DIFF_TQ, LANES), lambda b, h, i: (b * nq + i, h)),
                  pl.BlockSpec((seq, LANES), lambda b, h, i: (b, h)),
                  pl.BlockSpec((nk, hd, DIFF_TK), lambda b, h, i: (b, h, 0)),
                  pl.BlockSpec((4, HEAD_DIM), lambda b, h, i: (0, 0)),
                  pl.BlockSpec((hd, 1), lambda b, h, i: (0, 0))],
        out_specs=pl.BlockSpec((DIFF_TQ, LANES), lambda b, h, i: (b * nq + i, h)),
        out_shape=jax.ShapeDtypeStruct((t, C_HEADS * hd), BF16),
        scratch_shapes=[pltpu.VMEM((2, 1, DIFF_TQ), F32), pltpu.VMEM((2, 1, DIFF_TQ), F32),
                        pltpu.VMEM((2, hd, DIFF_TQ), F32)],
        compiler_params=_params("parallel", "parallel", "arbitrary"),
        name="diff_attn",
    )(qr, kr, vt, diff_lambda.astype(F32), subln_g.reshape(hd, 1).astype(F32))


def _merge_kernel(ya_ref, yb_ref, yc_ref, gl0_ref, gl1_ref, gl2_ref, gb_ref, wb_ref, wo_ref, h_ref, g_ref, b_ref,
                  o_ref, ob_ref):
    merged = None
    for n, (y_ref, gl_ref) in enumerate(((ya_ref, gl0_ref), (yb_ref, gl1_ref), (yc_ref, gl2_ref))):
        proj = jnp.dot(y_ref[...], wb_ref[n], preferred_element_type=F32)
        gate = jax.nn.sigmoid(gl_ref[...].astype(F32) + gb_ref[n:n + 1, :])
        merged = gate * proj if merged is None else merged + gate * proj
    mix = jnp.dot(merged.astype(BF16), wo_ref[...], preferred_element_type=F32)
    y = _layer_norm(DEEPNORM_ALPHA * h_ref[...] + mix, g_ref[...], b_ref[...])
    o_ref[...] = y
    ob_ref[...] = y.astype(ob_ref.dtype)


def _merge(ya, yb, yc, z, gate_bias, w_branch, w_out, h, ln_g, ln_b, tt=512):
    t = h.shape[0]
    d = D_MODEL
    bw = BRANCH_WIDTH
    ytile = pl.BlockSpec((tt, bw), lambda i: (i, 0))
    gl = lambda n: pl.BlockSpec((tt, d), lambda i: (i, COL_GL // d + n))
    full = lambda shape: pl.BlockSpec(shape, lambda i: (0,) * len(shape))
    return pl.pallas_call(
        _merge_kernel,
        grid=(t // tt,),
        in_specs=[ytile, ytile, ytile, gl(0), gl(1), gl(2), full((N_BRANCHES, d)), full((N_BRANCHES, bw, d)),
                  full((d, d)), pl.BlockSpec((tt, d), lambda i: (i, 0)), full((1, d)), full((1, d))],
        out_specs=[pl.BlockSpec((tt, d), lambda i: (i, 0)), pl.BlockSpec((tt, d), lambda i: (i, 0))],
        out_shape=[jax.ShapeDtypeStruct((t, d), F32), jax.ShapeDtypeStruct((t, d), BF16)],
        compiler_params=_params("parallel"),
        name="merge_out_ln",
    )(ya, yb, yc, z, z, z, gate_bias.astype(F32), w_branch.astype(BF16), w_out.astype(BF16), h,
      ln_g.reshape(1, d).astype(F32), ln_b.reshape(1, d).astype(F32))


def _top_rows(s, k, with_rank=False):
    rows = []
    cur = s
    rank = jnp.full(s.shape, float(k), F32) if with_rank else None
    for r in range(k):
        m = jnp.max(cur, axis=0, keepdims=True)
        rows.append(m)
        hit = cur == m
        if with_rank:
            rank = jnp.where(hit, float(r), rank)
        cur = jnp.where(hit, -jnp.inf, cur)
    return (rows, rank) if with_rank else rows


def _batcher_pairs(n):
    pairs = []
    p = 1
    while p < n:
        k = p
        while k >= 1:
            for j in range(k % p, n - k, 2 * k):
                for i in range(min(k, n - j - k)):
                    if (i + j) // (2 * p) == (i + j + k) // (2 * p):
                        pairs.append((i + j, i + j + k))
            k //= 2
        p *= 2
    return pairs


def _sorted_top(s, k):
    sub = s.shape[0] // k
    x = [s[i * sub:(i + 1) * sub, :] for i in range(k)]

    def exchange(lo, hi):
        x[lo], x[hi] = jnp.maximum(x[lo], x[hi]), jnp.minimum(x[lo], x[hi])

    for lo, hi in _batcher_pairs(k):
        exchange(lo, hi)
    shift = sub // 2
    while shift >= 1:
        y = [pltpu.roll(v, shift, 0) for v in x]
        x = [jnp.maximum(x[i], y[k - 1 - i]) for i in range(k)]
        stride = k // 2
        while stride >= 1:
            for i in range(k):
                if i & stride == 0:
                    exchange(i, i + stride)
            stride //= 2
        shift //= 2
    return x


def _count_prefix(a, pred):
    c8 = pred(a[7])
    c4 = pred(jnp.where(c8, a[11], a[3]))
    c2 = pred(jnp.where(c8, jnp.where(c4, a[13], a[9]), jnp.where(c4, a[5], a[1])))
    hi = jnp.where(c4, jnp.where(c2, a[14], a[12]), jnp.where(c2, a[10], a[8]))
    lo = jnp.where(c4, jnp.where(c2, a[6], a[4]), jnp.where(c2, a[2], a[0]))
    c1 = pred(jnp.where(c8, hi, lo))
    c0 = pred(a[15])
    one = lambda c, v: jnp.where(c, v, 0.0)
    return one(c8, 8.0) + one(c4, 4.0) + one(c2, 2.0) + one(c1, 1.0) + one(c0, 1.0)


def _pair_word(x):
    bits = lax.bitcast_convert_type(x.astype(BF16).astype(F32), jnp.uint32)
    return bits | (bits >> 16)


def _route_kernel(h_ref, wq_ref, keys_ref, rk_ref, e1_ref, cnt_ref, e2_ref):
    qt = lax.dot_general(wq_ref[...], h_ref[...], _NT, preferred_element_type=F32).astype(BF16)
    for h in range(PEER_HEADS):
        sc = []
        for p in range(2):
            r0 = (h * 2 + p) * PEER_HALF
            sc.append(jnp.dot(keys_ref[h, p], qt[r0:r0 + PEER_HALF, :], preferred_element_type=F32))
        s1, s2 = sc
        a8 = _sorted_top(s1, PEER_TOPK)
        b8 = _sorted_top(s2, PEER_TOPK)
        a = [v[0:1, :] for v in a8]
        b = [v[0:1, :] for v in b8]
        cand = jnp.concatenate([a[k] + b[l] for k in range(PEER_TOPK) for l in range(PEER_TOPK // (k + 1))], axis=0)
        tau = _top_rows(cand, PEER_TOPK)[PEER_TOPK - 1]
        top = a[0] + b[0]
        zsum = jnp.sum(jnp.where(cand >= tau, jnp.exp(cand - top), 0.0), axis=0, keepdims=True)
        tau8 = jnp.broadcast_to(tau, a8[0].shape)
        sub = a8[0].shape[0]
        slabs = lambda s: [s[i * sub:(i + 1) * sub, :] for i in range(s.shape[0] // sub)]
        rank1 = jnp.concatenate([_count_prefix(a8, lambda v: v > x) for x in slabs(s1)], axis=0)
        cnt = jnp.concatenate([_count_prefix(a8, lambda v: v + x >= tau8) for x in slabs(s2)], axis=0)
        rk_ref[h] = _pair_word(rank1)
        e1_ref[h] = _pair_word(jnp.exp(s1 - a[0]) / zsum)
        cnt_ref[h] = cnt.astype(cnt_ref.dtype)
        e2_ref[h] = jnp.exp(s2 - b[0]).astype(e2_ref.dtype)


def _peer_route(hb, wq_t, keys, tt=256):
    t = hb.shape[0]
    shape = (PEER_HEADS, PEER_N_KEYS, t)
    bspec = pl.BlockSpec((PEER_HEADS, PEER_N_KEYS, tt), lambda i: (0, 0, i))
    return pl.pallas_call(
        _route_kernel,
        grid=(t // tt,),
        in_specs=[pl.BlockSpec((tt, D_MODEL), lambda i: (i, 0)),
                  pl.BlockSpec(wq_t.shape, lambda i: (0, 0)),
                  pl.BlockSpec(keys.shape, lambda i: (0, 0, 0, 0))],
        out_specs=[bspec, bspec, bspec, bspec],
        out_shape=[jax.ShapeDtypeStruct(shape, jnp.uint32), jax.ShapeDtypeStruct(shape, jnp.uint32),
                   jax.ShapeDtypeStruct(shape, BF16), jax.ShapeDtypeStruct(shape, BF16)],
        compiler_params=_params("parallel"),
        name="peer_route",
    )(hb, wq_t, keys)


PEER_TT = 512
PEER_IG = 8
PEER_EB = PEER_IG * PEER_N_KEYS
PEER_NB = 2
PEER_E = PEER_NB * PEER_EB
PEER_JB = 64
GELU_C1 = -2.0 * math.sqrt(2.0 / math.pi) * LOG2E
GELU_C2 = GELU_C1 * 0.044715


def _gelu_tanh(x):
    return x * (1.0 / (1.0 + jnp.exp2(x * (GELU_C1 + GELU_C2 * (x * x)))))


def _peer_kernel(hb_ref, u_ref, vt_ref, rk_ref, cnt_ref, e1_ref, e2_ref, h_ref, g_ref, b_ref,
                 o_ref, ob_ref, acc_ref, p_ref):
    e = pl.program_id(1)
    tt = hb_ref.shape[0]

    @pl.when(e == 0)
    def _():
        acc_ref[...] = jnp.zeros_like(acc_ref)

    def act_rows(k, r0, r1):
        a = lax.dot_general(u_ref[k * PEER_EB + r0:k * PEER_EB + r1, :], hb_ref[...], _NT,
                            preferred_element_type=F32)
        p_ref[k, r0:r1, :] = _gelu_tanh(a.astype(p_ref.dtype))

    def row_pairs(ref, h, k, ii, ls):
        return pltpu.bitcast(jnp.broadcast_to(ref[h, k, ii:ii + 1, ls], (PEER_JB // 2, LANES)), BF16)

    def gate_cols(k, c0, c1):
        for lc in range(c0 // LANES, c1 // LANES):
            ls = slice(lc * LANES, (lc + 1) * LANES)
            for jb in range(PEER_N_KEYS // PEER_JB):
                js = slice(jb * PEER_JB, (jb + 1) * PEER_JB)
                w = [jnp.zeros((PEER_JB, LANES), BF16) for _ in range(PEER_IG)]
                for h in range(PEER_HEADS):
                    cnt = cnt_ref[h, js, ls]
                    e2 = e2_ref[h, js, ls]
                    for ii in range(PEER_IG):
                        gate = row_pairs(e1_ref, h, k, ii, ls) * e2
                        w[ii] = jnp.where(row_pairs(rk_ref, h, k, ii, ls) < cnt, w[ii] + gate, w[ii])
                for ii in range(PEER_IG):
                    rs = slice(ii * PEER_N_KEYS + jb * PEER_JB, ii * PEER_N_KEYS + (jb + 1) * PEER_JB)
                    p_ref[k, rs, ls] = w[ii] * p_ref[k, rs, ls]

    def out_cols(k, c0, c1):
        acc_ref[:, c0:c1] += jnp.dot(vt_ref[:, k * PEER_EB:(k + 1) * PEER_EB], p_ref[k, :, c0:c1],
                                     preferred_element_type=F32)

    act_rows(0, 0, PEER_EB)
    for k in range(PEER_NB):
        if k + 1 < PEER_NB:
            act_rows(k + 1, 0, PEER_EB)
        gate_cols(k, 0, tt)
        out_cols(k, 0, tt)

    @pl.when(e == pl.num_programs(1) - 1)
    def _():
        ff = acc_ref[...].T
        y = _layer_norm(DEEPNORM_ALPHA * h_ref[...] + ff, g_ref[...], b_ref[...])
        o_ref[...] = y
        ob_ref[...] = y.astype(ob_ref.dtype)


def _peer_dense(hb, h, u_b, vt_b, rk, e1, cnt, e2, ln_g, ln_b):
    t = h.shape[0]
    d = D_MODEL
    tt = min(PEER_TT, t)
    ne = PEER_N_EXPERTS // PEER_E
    big = pl.BlockSpec((PEER_HEADS, PEER_N_KEYS, tt), lambda i, e: (0, 0, i))
    grp = pl.BlockSpec((PEER_HEADS, PEER_NB, PEER_IG, tt), lambda i, e: (0, e, 0, i))
    row = pl.BlockSpec((1, d), lambda i, e: (0, 0))
    tok = pl.BlockSpec((tt, d), lambda i, e: (i, 0))
    grouped = lambda a: a.reshape(PEER_HEADS, PEER_N_KEYS // PEER_IG, PEER_IG, t)
    return pl.pallas_call(
        _peer_kernel,
        grid=(t // tt, ne),
        in_specs=[tok, pl.BlockSpec((PEER_E, d), lambda i, e: (e, 0)), pl.BlockSpec((d, PEER_E), lambda i, e: (0, e)),
                  grp, big, grp, big, tok, row, row],
        out_specs=[pl.BlockSpec((tt, d), lambda i, e: (i, 0)), pl.BlockSpec((tt, d), lambda i, e: (i, 0))],
        out_shape=[jax.ShapeDtypeStruct((t, d), F32), jax.ShapeDtypeStruct((t, d), BF16)],
        scratch_shapes=[pltpu.VMEM((d, tt), F32), pltpu.VMEM((PEER_NB, PEER_EB, tt), BF16)],
        compiler_params=_params("parallel", "arbitrary"),
        name="peer_dense",
    )(hb, u_b, vt_b, grouped(rk), cnt, grouped(e1), e2, h, ln_g.reshape(1, d).astype(F32),
      ln_b.reshape(1, d).astype(F32))


def kernel(x, positions, w_in, gate_bias, rel_bias, conv_w, conv_b, lru_w_r, lru_b_r, lru_w_i, lru_b_i, lru_lambda,
           diff_lambda, diff_subln_g, w_branch, w_out, ln1_g, ln1_b, peer_w_q, peer_sub_keys, peer_u, peer_v,
           ln2_g, ln2_b):
    batch, seq, d = x.shape
    t = batch * seq
    h = x.reshape(t, d).astype(F32)
    hb = h.astype(BF16)
    pos_col = positions.reshape(t, 1).astype(jnp.int32)
    inv_freq = jnp.power(ROPE_THETA, -jnp.arange(0, HEAD_DIM, 2, dtype=F32) / HEAD_DIM)
    inv_tiled = jnp.tile(inv_freq, LANES // (HEAD_DIM // 2)).reshape(1, LANES)

    for l in range(DEPTH):
        lam_init = 0.8 - 0.6 * math.exp(-0.3 * l)
        z = _matmul(hb, w_in[l].astype(BF16), BF16)
        ya = _band_attention(z, _band_bias_table(rel_bias[l]), batch, seq)
        yb = _rglru(z, conv_w[l], conv_b[l], lru_w_r[l], lru_b_r[l], lru_w_i[l], lru_b_i[l], lru_lambda[l],
                    batch, seq)
        qr, kr = _rope(z, pos_col, inv_tiled)
        yc = _diff_attention(qr, kr, z, diff_lambda[l], diff_subln_g[l], lam_init, batch, seq)
        h, hb = _merge(ya, yb, yc, z, gate_bias[l], w_branch[l], w_out[l], h, ln1_g[l], ln1_b[l])
        rk, e1, cnt, e2 = _peer_route(hb, peer_w_q[l].T.astype(BF16), peer_sub_keys[l].astype(BF16))
        h, hb = _peer_dense(hb, h, peer_u[l].astype(BF16), peer_v[l].astype(BF16).T, rk, e1, cnt, e2,
                            ln2_g[l], ln2_b[l])
    return h.reshape(batch, seq, d)
```

```python
import functools
import math

import numpy as np
import jax
import jax.numpy as jnp
from jax import lax
from jax.experimental import pallas as pl
from jax.experimental.pallas import tpu as pltpu

D_MODEL = 1024
DEPTH = 2
CHUNK = 64
HEAD_DIM = 64
A_HEADS = 8
A_LOOKBACK = 8
A_MAX_REL = 256
LRU_WIDTH = 512
LRU_BLOCKS = 8
LRU_BLOCK_DIM = 64
CONV_WIDTH = 4
LRU_C = 8.0
C_HEADS = 4
ROPE_THETA = 10000.0
N_BRANCHES = 3
BRANCH_WIDTH = 512
PEER_HEADS = 8
PEER_N_KEYS = 128
PEER_N_EXPERTS = PEER_N_KEYS * PEER_N_KEYS
PEER_HALF = 128
PEER_TOPK = 16
DEEPNORM_ALPHA = (2 * DEPTH) ** 0.25
LN_EPS = 1e-5
NEG_INF = -1e30

LANES = 128
VMEM_LIMIT = 56 * 1024 * 1024

COL_AQ, COL_AK, COL_AV = 0, 512, 1024
COL_BX, COL_BG = 1536, 2048
COL_CQ, COL_CK, COL_CV = 2560, 3072, 3584
COL_GL = 4096
IN_COLS = 7168

F32 = jnp.float32
BF16 = jnp.bfloat16
_NT = (((1,), (1,)), ((), ()))


def _params(*sem):
    return pltpu.CompilerParams(dimension_semantics=sem, vmem_limit_bytes=VMEM_LIMIT)


def _layer_norm(y, g, b):
    mu = jnp.mean(y, axis=-1, keepdims=True)
    d = y - mu
    var = jnp.mean(d * d, axis=-1, keepdims=True)
    return d * lax.rsqrt(var + LN_EPS) * g + b


def _mm_kernel(x_ref, w_ref, o_ref):
    o_ref[...] = jnp.dot(x_ref[...], w_ref[...], preferred_element_type=F32).astype(o_ref.dtype)


def _matmul(x, w, out_dtype, tm=1024, tn=1792):
    m, k = x.shape
    n = w.shape[1]
    return pl.pallas_call(
        _mm_kernel,
        grid=(n // tn, m // tm),
        in_specs=[pl.BlockSpec((tm, k), lambda j, i: (i, 0)),
                  pl.BlockSpec((k, tn), lambda j, i: (0, j))],
        out_specs=pl.BlockSpec((tm, tn), lambda j, i: (i, j)),
        out_shape=jax.ShapeDtypeStruct((m, n), out_dtype),
        compiler_params=_params("parallel", "parallel"),
        name="in_proj",
    )(x, w)


def _rope_kernel(pos_ref, inv_ref, q_ref, k_ref, qo_ref, ko_ref):
    tt = pos_ref.shape[0]
    ang = pos_ref[...].astype(F32) * inv_ref[...]
    c = jnp.cos(ang)
    s = jnp.sin(ang)
    lane = lax.broadcasted_iota(jnp.int32, (tt, LANES), 1)
    first = (lane % HEAD_DIM) < (HEAD_DIM // 2)
    s_signed = jnp.where(first, -s, s)

    def rot(x):
        partner = jnp.where(first, pltpu.roll(x, LANES - HEAD_DIM // 2, 1), pltpu.roll(x, HEAD_DIM // 2, 1))
        return x * c + partner * s_signed

    for blk in range(q_ref.shape[1] // LANES):
        sl = slice(blk * LANES, (blk + 1) * LANES)
        qo_ref[:, sl] = (rot(q_ref[:, sl].astype(F32)) * (LOG2E * HEAD_DIM ** -0.5)).astype(qo_ref.dtype)
        ko_ref[:, sl] = rot(k_ref[:, sl].astype(F32)).astype(ko_ref.dtype)


def _rope(z, pos_col, inv_tiled, tt=512):
    t = z.shape[0]
    w = 512
    return pl.pallas_call(
        _rope_kernel,
        grid=(t // tt,),
        in_specs=[pl.BlockSpec((tt, 1), lambda i: (i, 0)),
                  pl.BlockSpec((1, LANES), lambda i: (0, 0)),
                  pl.BlockSpec((tt, w), lambda i: (i, COL_CQ // w)),
                  pl.BlockSpec((tt, w), lambda i: (i, COL_CK // w))],
        out_specs=[pl.BlockSpec((tt, w), lambda i: (i, 0)),
                   pl.BlockSpec((tt, w), lambda i: (i, 0))],
        out_shape=[jax.ShapeDtypeStruct((t, w), BF16), jax.ShapeDtypeStruct((t, w), BF16)],
        compiler_params=_params("parallel"),
        name="rope",
    )(pos_col, inv_tiled, z, z)


BAND_Q = 2 * CHUNK
BAND_K = (A_LOOKBACK + 2) * CHUNK
BAND_PAD = A_LOOKBACK * CHUNK


def _band_bias_table(rel_bias):
    rb = rel_bias.astype(F32)
    nh, rel = rb.shape
    far = jnp.broadcast_to(rb[:, -1:], (nh, BAND_PAD - A_MAX_REL))
    near = jnp.broadcast_to(rb[:, :1], (nh, BAND_K - (BAND_PAD - A_MAX_REL) - rel))
    wrap = jnp.broadcast_to(rb[:, -1:], (nh, BAND_Q))
    v = jnp.concatenate([far, rb[:, ::-1], near, wrap], axis=1)
    period = BAND_K + BAND_Q
    flat = jnp.tile(v, (1, BAND_Q))[:, :BAND_Q * (period - 1)]
    bias = flat.reshape(nh, BAND_Q, period - 1)[:, :, :BAND_K]
    qi = np.arange(BAND_Q)[:, None]
    kj = np.arange(BAND_K)[None, :]
    q_chunk = qi // CHUNK + A_LOOKBACK
    k_chunk = kj // CHUNK
    allowed = (k_chunk <= q_chunk) & (k_chunk >= q_chunk - A_LOOKBACK)
    return jnp.where(jnp.asarray(allowed)[None], bias, NEG_INF)


def _band_kernel(q_ref, k_ref, v_ref, tab_ref, o_ref, kp_ref, vp_ref):
    c = pl.program_id(2)

    @pl.when(c == 0)
    def _():
        zeros = jnp.zeros((BAND_PAD, LANES), kp_ref.dtype)
        kp_ref[0:BAND_PAD, :] = zeros
        vp_ref[0:BAND_PAD, :] = zeros
        kp_ref[BAND_PAD:, :] = k_ref[...]
        vp_ref[BAND_PAD:, :] = v_ref[...]

    lane = lax.broadcasted_iota(jnp.int32, (BAND_Q, LANES), 1)
    col = lax.broadcasted_iota(jnp.int32, (BAND_Q, BAND_K), 1)
    firsts = [(c * BAND_NBLK + blk) * BAND_Q for blk in range(BAND_NBLK)]
    starts = [pl.multiple_of(f, BAND_Q) for f in firsts]
    kws = [kp_ref[pl.ds(st, BAND_K), :] for st in starts]
    scores = []
    for blk in range(BAND_NBLK):
        q = q_ref[blk * BAND_Q:(blk + 1) * BAND_Q, :]
        for hh in range(2):
            keep = (lane < HEAD_DIM) if hh == 0 else (lane >= HEAD_DIM)
            qm = jnp.where(keep, q, jnp.zeros_like(q))
            scores.append(lax.dot_general(qm, kws[blk], _NT, preferred_element_type=F32))
    probs = []
    for blk in range(BAND_NBLK):
        valid = (col + firsts[blk]) >= BAND_PAD
        for hh in range(2):
            s = scores[2 * blk + hh] * (HEAD_DIM ** -0.5)
            s = jnp.where(valid, s + tab_ref[hh], NEG_INF)
            m = jnp.max(s, axis=-1, keepdims=True)
            e = jnp.exp(s - m)
            probs.append((e / jnp.sum(e, axis=-1, keepdims=True)).astype(kp_ref.dtype))
    for blk in range(BAND_NBLK):
        vw = vp_ref[pl.ds(starts[blk], BAND_K), :]
        outs = [jnp.dot(probs[2 * blk + hh], vw, preferred_element_type=F32) for hh in range(2)]
        o_ref[blk * BAND_Q:(blk + 1) * BAND_Q, :] = jnp.where(lane < HEAD_DIM, outs[0], outs[1]).astype(o_ref.dtype)


BAND_NBLK = 8


def _band_attention(z, table, batch, seq):
    t = z.shape[0]
    nq = seq // (BAND_Q * BAND_NBLK)
    return pl.pallas_call(
        _band_kernel,
        grid=(batch, A_HEADS // 2, nq),
        in_specs=[pl.BlockSpec((BAND_Q * BAND_NBLK, LANES), lambda b, g, c: (b * nq + c, COL_AQ // LANES + g)),
                  pl.BlockSpec((seq, LANES), lambda b, g, c: (b, COL_AK // LANES + g)),
                  pl.BlockSpec((seq, LANES), lambda b, g, c: (b, COL_AV // LANES + g)),
                  pl.BlockSpec((2, BAND_Q, BAND_K), lambda b, g, c: (g, 0, 0))],
        out_specs=pl.BlockSpec((BAND_Q * BAND_NBLK, LANES), lambda b, g, c: (b * nq + c, g)),
        out_shape=jax.ShapeDtypeStruct((t, A_HEADS * HEAD_DIM), BF16),
        scratch_shapes=[pltpu.VMEM((seq + BAND_PAD, LANES), z.dtype),
                        pltpu.VMEM((seq + BAND_PAD, LANES), z.dtype)],
        compiler_params=_params("parallel", "parallel", "arbitrary"),
        name="band_attn",
    )(z, z, z, table)


LRU_ROWS = 8


def _lru_kernel(bx_ref, bg_ref, cw_ref, cb_ref, wr_ref, br_ref, wi_ref, bi_ref, lam_ref, o_ref,
                tail_ref, h_ref, a_ref, u_ref):
    tt = bx_ref.shape[0]

    @pl.when(pl.program_id(1) == 0)
    def _():
        tail_ref[...] = jnp.zeros_like(tail_ref)
        h_ref[...] = jnp.zeros_like(h_ref)

    x = bx_ref[...].astype(F32)
    xx = jnp.concatenate([tail_ref[...], x], axis=0)
    tail_ref[...] = x[tt - 8:, :]
    xc = cb_ref[...] + xx[5:5 + tt, :] * cw_ref[0:1, :]
    for tap in range(1, CONV_WIDTH):
        xc = xc + xx[5 + tap:5 + tap + tt, :] * cw_ref[tap:tap + 1, :]
    xcb = xc.astype(BF16)
    r = jax.nn.sigmoid(jnp.dot(xcb, wr_ref[...], preferred_element_type=F32) + br_ref[...])
    i = jax.nn.sigmoid(jnp.dot(xcb, wi_ref[...], preferred_element_type=F32) + bi_ref[...])
    log_a = LRU_C * r * jax.nn.log_sigmoid(lam_ref[...])
    a = jnp.exp(log_a)
    a_ref[...] = a
    u_ref[...] = jnp.sqrt(1.0 - a * a) * (i * xc)

    def group(gi, h):
        base = pl.multiple_of(gi * LRU_ROWS, LRU_ROWS)
        for rr in range(LRU_ROWS):
            h = a_ref[pl.ds(base + rr, 1), :] * h + u_ref[pl.ds(base + rr, 1), :]
            u_ref[pl.ds(base + rr, 1), :] = h
        return h

    h_ref[...] = lax.fori_loop(0, tt // LRU_ROWS, group, h_ref[...])
    o_ref[...] = (u_ref[...] * jax.nn.gelu(bg_ref[...].astype(F32))).astype(o_ref.dtype)


def _block_diag(w):
    eye = jnp.eye(LRU_BLOCKS, dtype=w.dtype)
    return jnp.einsum('gij,gh->gihj', w, eye).reshape(LRU_WIDTH, LRU_WIDTH)


def _rglru(z, conv_w, conv_b, w_r, b_r, w_i, b_i, lru_lambda, batch, seq, tt=512):
    t = z.shape[0]
    nt = seq // tt
    w = LRU_WIDTH
    row = lambda a: a.reshape(1, w).astype(F32)
    full = lambda shape: pl.BlockSpec(shape, lambda b, j: (0,) * len(shape))
    return pl.pallas_call(
        _lru_kernel,
        grid=(batch, nt),
        in_specs=[pl.BlockSpec((tt, w), lambda b, j: (b * nt + j, COL_BX // w)),
                  pl.BlockSpec((tt, w), lambda b, j: (b * nt + j, COL_BG // w)),
                  full((CONV_WIDTH, w)), full((1, w)), full((w, w)), full((1, w)), full((w, w)), full((1, w)),
                  full((1, w))],
        out_specs=pl.BlockSpec((tt, w), lambda b, j: (b * nt + j, 0)),
        out_shape=jax.ShapeDtypeStruct((t, w), BF16),
        scratch_shapes=[pltpu.VMEM((8, w), F32), pltpu.VMEM((1, w), F32),
                        pltpu.VMEM((tt, w), F32), pltpu.VMEM((tt, w), F32)],
        compiler_params=_params("parallel", "arbitrary"),
        name="rglru",
    )(z, z, conv_w.astype(F32), row(conv_b), _block_diag(w_r).astype(BF16), row(b_r),
      _block_diag(w_i).astype(BF16), row(b_i), row(lru_lambda))


DIFF_TQ = 1024
DIFF_TK = 1024
LOG2E = 1.4426950408889634
DIFF_NSUB = 4


def _diff_kernel(q_ref, k_ref, v_ref, dl_ref, g_ref, o_ref, m_ref, l_ref, acc_ref, *, lam_init):
    qi = pl.program_id(2)
    tq, tk = DIFF_TQ, DIFF_TK
    q = q_ref[...]
    lane = lax.broadcasted_iota(jnp.int32, (tq, LANES), 1)
    qs = (jnp.where(lane < HEAD_DIM, q, jnp.zeros_like(q)), jnp.where(lane >= HEAD_DIM, q, jnp.zeros_like(q)))
    m_ref[...] = jnp.full(m_ref.shape, NEG_INF, F32)
    l_ref[...] = jnp.zeros(l_ref.shape, F32)
    acc_ref[...] = jnp.zeros(acc_ref.shape, F32)

    chains = [(sub, mp) for sub in range(DIFF_NSUB) for mp in range(2)]
    tsub = tq // DIFF_NSUB

    def blocks(js, masked):
        staged = []
        for j in js:
            start = pl.multiple_of(j * tk, tk)
            kb = k_ref[pl.ds(start, tk), :]
            scores = [lax.dot_general(kb, qs[mp][sub * tsub:(sub + 1) * tsub], _NT, preferred_element_type=F32)
                      for sub, mp in chains]
            staged.append((v_ref[j], scores))
        for idx, (vtb, scores) in enumerate(staged):
            diagonal = masked and idx == len(staged) - 1
            weights = []
            for (sub, mp), s in zip(chains, scores):
                cols = slice(sub * tsub, (sub + 1) * tsub)
                if diagonal:
                    key = lax.broadcasted_iota(jnp.int32, (tk, tsub), 0)
                    qry = lax.broadcasted_iota(jnp.int32, (tk, tsub), 1) + sub * tsub
                    s = jnp.where((key // CHUNK) <= (qry // CHUNK), s, NEG_INF)
                m_old = m_ref[mp, :, cols]
                m_new = jnp.maximum(m_old, jnp.max(s, axis=0, keepdims=True))
                alpha = jnp.exp2(m_old - m_new)
                p = jnp.exp2(s - m_new)
                l_ref[mp, :, cols] = alpha * l_ref[mp, :, cols] + jnp.sum(p, axis=0, keepdims=True)
                m_ref[mp, :, cols] = m_new
                weights.append((alpha, p.astype(vtb.dtype)))
            for (sub, mp), (alpha, p) in zip(chains, weights):
                cols = slice(sub * tsub, (sub + 1) * tsub)
                acc_ref[mp, :, cols] = alpha * acc_ref[mp, :, cols] + jnp.dot(vtb, p, preferred_element_type=F32)

    def pair(i, carry):
        blocks([2 * i, 2 * i + 1], False)
        return carry

    lax.fori_loop(0, qi // 2, pair, 0)

    @pl.when(qi % 2 == 1)
    def _():
        blocks([qi - 1, qi], True)

    @pl.when(qi % 2 == 0)
    def _():
        blocks([qi], True)

    dl = dl_ref[...]
    lam = (jnp.exp(jnp.sum(dl[0:1] * dl[1:2], axis=-1, keepdims=True))
           - jnp.exp(jnp.sum(dl[2:3] * dl[3:4], axis=-1, keepdims=True)) + lam_init)
    o = acc_ref[0] / l_ref[0] - lam * (acc_ref[1] / l_ref[1])
    ms = jnp.mean(o * o, axis=0, keepdims=True)
    o = o * lax.rsqrt(ms + LN_EPS) * g_ref[...] * (1.0 - lam_init)
    o_ref[...] = o.T.astype(o_ref.dtype)


def _diff_attention(qr, kr, z, diff_lambda, subln_g, lam_init, batch, seq):
    t = z.shape[0]
    nq = seq // DIFF_TQ
    nk = seq // DIFF_TK
    hd = 2 * HEAD_DIM
    vt = z[:, COL_CV:COL_CV + C_HEADS * hd].reshape(t // DIFF_TK, DIFF_TK, C_HEADS * hd).transpose(0, 2, 1)
    return pl.pallas_call(
        functools.partial(_diff_kernel, lam_init=lam_init),
        grid=(batch, C_HEADS, nq),
        in_specs=[pl.BlockSpec((DIFF_TQ, LANES), lambda b, h, i: (b * nq + i, h)),
                  pl.BlockSpec((seq, LANES), lambda b, h, i: (b, h)),
                  pl.BlockSpec((nk, hd, DIFF_TK), lambda b, h, i: (b, h, 0)),
                  pl.BlockSpec((4, HEAD_DIM), lambda b, h, i: (0, 0)),
                  pl.BlockSpec((hd, 1), lambda b, h, i: (0, 0))],
        out_specs=pl.BlockSpec((DIFF_TQ, LANES), lambda b, h, i: (b * nq + i, h)),
        out_shape=jax.ShapeDtypeStruct((t, C_HEADS * hd), BF16),
        scratch_shapes=[pltpu.VMEM((2, 1, DIFF_TQ), F32), pltpu.VMEM((2, 1, DIFF_TQ), F32),
                        pltpu.VMEM((2, hd, DIFF_TQ), F32)],
        compiler_params=_params("parallel", "parallel", "arbitrary"),
        name="diff_attn",
    )(qr, kr, vt, diff_lambda.astype(F32), subln_g.reshape(hd, 1).astype(F32))


def _merge_kernel(ya_ref, yb_ref, yc_ref, gl0_ref, gl1_ref, gl2_ref, gb_ref, wb_ref, wo_ref, h_ref, g_ref, b_ref,
                  o_ref, ob_ref):
    merged = None
    for n, (y_ref, gl_ref) in enumerate(((ya_ref, gl0_ref), (yb_ref, gl1_ref), (yc_ref, gl2_ref))):
        proj = jnp.dot(y_ref[...], wb_ref[n], preferred_element_type=F32)
        gate = jax.nn.sigmoid(gl_ref[...].astype(F32) + gb_ref[n:n + 1, :])
        merged = gate * proj if merged is None else merged + gate * proj
    mix = jnp.dot(merged.astype(BF16), wo_ref[...], preferred_element_type=F32)
    y = _layer_norm(DEEPNORM_ALPHA * h_ref[...] + mix, g_ref[...], b_ref[...])
    o_ref[...] = y
    ob_ref[...] = y.astype(ob_ref.dtype)


def _merge(ya, yb, yc, z, gate_bias, w_branch, w_out, h, ln_g, ln_b, tt=512):
    t = h.shape[0]
    d = D_MODEL
    bw = BRANCH_WIDTH
    ytile = pl.BlockSpec((tt, bw), lambda i: (i, 0))
    gl = lambda n: pl.BlockSpec((tt, d), lambda i: (i, COL_GL // d + n))
    full = lambda shape: pl.BlockSpec(shape, lambda i: (0,) * len(shape))
    return pl.pallas_call(
        _merge_kernel,
        grid=(t // tt,),
        in_specs=[ytile, ytile, ytile, gl(0), gl(1), gl(2), full((N_BRANCHES, d)), full((N_BRANCHES, bw, d)),
                  full((d, d)), pl.BlockSpec((tt, d), lambda i: (i, 0)), full((1, d)), full((1, d))],
        out_specs=[pl.BlockSpec((tt, d), lambda i: (i, 0)), pl.BlockSpec((tt, d), lambda i: (i, 0))],
        out_shape=[jax.ShapeDtypeStruct((t, d), F32), jax.ShapeDtypeStruct((t, d), BF16)],
        compiler_params=_params("parallel"),
        name="merge_out_ln",
    )(ya, yb, yc, z, z, z, gate_bias.astype(F32), w_branch.astype(BF16), w_out.astype(BF16), h,
      ln_g.reshape(1, d).astype(F32), ln_b.reshape(1, d).astype(F32))


def _top_rows(s, k, with_rank=False):
    rows = []
    cur = s
    rank = jnp.full(s.shape, float(k), F32) if with_rank else None
    for r in range(k):
        m = jnp.max(cur, axis=0, keepdims=True)
        rows.append(m)
        hit = cur == m
        if with_rank:
            rank = jnp.where(hit, float(r), rank)
        cur = jnp.where(hit, -jnp.inf, cur)
    return (rows, rank) if with_rank else rows


def _batcher_pairs(n):
    pairs = []
    p = 1
    while p < n:
        k = p
        while k >= 1:
            for j in range(k % p, n - k, 2 * k):
                for i in range(min(k, n - j - k)):
                    if (i + j) // (2 * p) == (i + j + k) // (2 * p):
                        pairs.append((i + j, i + j + k))
            k //= 2
        p *= 2
    return pairs


def _sorted_top(s, k):
    sub = s.shape[0] // k
    x = [s[i * sub:(i + 1) * sub, :] for i in range(k)]

    def exchange(lo, hi):
        x[lo], x[hi] = jnp.maximum(x[lo], x[hi]), jnp.minimum(x[lo], x[hi])

    for lo, hi in _batcher_pairs(k):
        exchange(lo, hi)
    shift = sub // 2
    while shift >= 1:
        y = [pltpu.roll(v, shift, 0) for v in x]
        x = [jnp.maximum(x[i], y[k - 1 - i]) for i in range(k)]
        stride = k // 2
        while stride >= 1:
            for i in range(k):
                if i & stride == 0:
                    exchange(i, i + stride)
            stride //= 2
        shift //= 2
    return x


def _count_prefix(a, pred):
    c8 = pred(a[7])
    c4 = pred(jnp.where(c8, a[11], a[3]))
    c2 = pred(jnp.where(c8, jnp.where(c4, a[13], a[9]), jnp.where(c4, a[5], a[1])))
    hi = jnp.where(c4, jnp.where(c2, a[14], a[12]), jnp.where(c2, a[10], a[8]))
    lo = jnp.where(c4, jnp.where(c2, a[6], a[4]), jnp.where(c2, a[2], a[0]))
    c1 = pred(jnp.where(c8, hi, lo))
    c0 = pred(a[15])
    one = lambda c, v: jnp.where(c, v, 0.0)
    return one(c8, 8.0) + one(c4, 4.0) + one(c2, 2.0) + one(c1, 1.0) + one(c0, 1.0)


def _pair_word(x):
    bits = lax.bitcast_convert_type(x.astype(BF16).astype(F32), jnp.uint32)
    return bits | (bits >> 16)


def _route_kernel(h_ref, wq_ref, keys_ref, rk_ref, e1_ref, cnt_ref, e2_ref):
    qt = lax.dot_general(wq_ref[...], h_ref[...], _NT, preferred_element_type=F32).astype(BF16)
    for h in range(PEER_HEADS):
        sc = []
        for p in range(2):
            r0 = (h * 2 + p) * PEER_HALF
            sc.append(jnp.dot(keys_ref[h, p], qt[r0:r0 + PEER_HALF, :], preferred_element_type=F32))
        s1, s2 = sc
        a8 = _sorted_top(s1, PEER_TOPK)
        b8 = _sorted_top(s2, PEER_TOPK)
        a = [v[0:1, :] for v in a8]
        b = [v[0:1, :] for v in b8]
        cand = jnp.concatenate([a[k] + b[l] for k in range(PEER_TOPK) for l in range(PEER_TOPK // (k + 1))], axis=0)
        tau = _top_rows(cand, PEER_TOPK)[PEER_TOPK - 1]
        top = a[0] + b[0]
        zsum = jnp.sum(jnp.where(cand >= tau, jnp.exp(cand - top), 0.0), axis=0, keepdims=True)
        tau8 = jnp.broadcast_to(tau, a8[0].shape)
        sub = a8[0].shape[0]
        slabs = lambda s: [s[i * sub:(i + 1) * sub, :] for i in range(s.shape[0] // sub)]
        rank1 = jnp.concatenate([_count_prefix(a8, lambda v: v > x) for x in slabs(s1)], axis=0)
        cnt = jnp.concatenate([_count_prefix(a8, lambda v: v + x >= tau8) for x in slabs(s2)], axis=0)
        rk_ref[h] = _pair_word(rank1)
        e1_ref[h] = _pair_word(jnp.exp(s1 - a[0]) / zsum)
        cnt_ref[h] = cnt.astype(cnt_ref.dtype)
        e2_ref[h] = jnp.exp(s2 - b[0]).astype(e2_ref.dtype)


def _peer_route(hb, wq_t, keys, tt=256):
    t = hb.shape[0]
    shape = (PEER_HEADS, PEER_N_KEYS, t)
    bspec = pl.BlockSpec((PEER_HEADS, PEER_N_KEYS, tt), lambda i: (0, 0, i))
    return pl.pallas_call(
        _route_kernel,
        grid=(t // tt,),
        in_specs=[pl.BlockSpec((tt, D_MODEL), lambda i: (i, 0)),
                  pl.BlockSpec(wq_t.shape, lambda i: (0, 0)),
                  pl.BlockSpec(keys.shape, lambda i: (0, 0, 0, 0))],
        out_specs=[bspec, bspec, bspec, bspec],
        out_shape=[jax.ShapeDtypeStruct(shape, jnp.uint32), jax.ShapeDtypeStruct(shape, jnp.uint32),
                   jax.ShapeDtypeStruct(shape, BF16), jax.ShapeDtypeStruct(shape, BF16)],
        compiler_params=_params("parallel"),
        name="peer_route",
    )(hb, wq_t, keys)


PEER_TT = 512
PEER_IG = 8
PEER_EB = PEER_IG * PEER_N_KEYS
PEER_NB = 2
PEER_E = PEER_NB * PEER_EB
PEER_JB = 64
GELU_C1 = -2.0 * math.sqrt(2.0 / math.pi) * LOG2E
GELU_C2 = GELU_C1 * 0.044715


def _gelu_tanh(x):
    return x * (1.0 / (1.0 + jnp.exp2(x * (GELU_C1 + GELU_C2 * (x * x)))))


def _peer_kernel(hb_ref, u_ref, vt_ref, rk_ref, cnt_ref, e1_ref, e2_ref, h_ref, g_ref, b_ref,
                 o_ref, ob_ref, acc_ref, p_ref):
    e = pl.program_id(1)
    tt = hb_ref.shape[0]

    @pl.when(e == 0)
    def _():
        acc_ref[...] = jnp.zeros_like(acc_ref)

    def act_rows(k, r0, r1):
        a = lax.dot_general(u_ref[k * PEER_EB + r0:k * PEER_EB + r1, :], hb_ref[...], _NT,
                            preferred_element_type=F32)
        p_ref[k, r0:r1, :] = _gelu_tanh(a.astype(p_ref.dtype))

    def row_pairs(ref, h, k, ii, ls):
        return pltpu.bitcast(jnp.broadcast_to(ref[h, k, ii:ii + 1, ls], (PEER_JB // 2, LANES)), BF16)

    def gate_cols(k, c0, c1):
        for lc in range(c0 // LANES, c1 // LANES):
            ls = slice(lc * LANES, (lc + 1) * LANES)
            for jb in range(PEER_N_KEYS // PEER_JB):
                js = slice(jb * PEER_JB, (jb + 1) * PEER_JB)
                w = [jnp.zeros((PEER_JB, LANES), BF16) for _ in range(PEER_IG)]
                for h in range(PEER_HEADS):
                    cnt = cnt_ref[h, js, ls]
                    e2 = e2_ref[h, js, ls]
                    for ii in range(PEER_IG):
                        gate = row_pairs(e1_ref, h, k, ii, ls) * e2
                        w[ii] = jnp.where(row_pairs(rk_ref, h, k, ii, ls) < cnt, w[ii] + gate, w[ii])
                for ii in range(PEER_IG):
                    rs = slice(ii * PEER_N_KEYS + jb * PEER_JB, ii * PEER_N_KEYS + (jb + 1) * PEER_JB)
                    p_ref[k, rs, ls] = w[ii] * p_ref[k, rs, ls]

    def out_cols(k, c0, c1):
        acc_ref[:, c0:c1] += jnp.dot(vt_ref[:, k * PEER_EB:(k + 1) * PEER_EB], p_ref[k, :, c0:c1],
                                     preferred_element_type=F32)

    act_rows(0, 0, PEER_EB)
    for k in range(PEER_NB):
        if k + 1 < PEER_NB:
            act_rows(k + 1, 0, PEER_EB)
        gate_cols(k, 0, tt)
        out_cols(k, 0, tt)

    @pl.when(e == pl.num_programs(1) - 1)
    def _():
        ff = acc_ref[...].T
        y = _layer_norm(DEEPNORM_ALPHA * h_ref[...] + ff, g_ref[...], b_ref[...])
        o_ref[...] = y
        ob_ref[...] = y.astype(ob_ref.dtype)


def _peer_dense(hb, h, u_b, vt_b, rk, e1, cnt, e2, ln_g, ln_b):
    t = h.shape[0]
    d = D_MODEL
    tt = min(PEER_TT, t)
    ne = PEER_N_EXPERTS // PEER_E
    big = pl.BlockSpec((PEER_HEADS, PEER_N_KEYS, tt), lambda i, e: (0, 0, i))
    grp = pl.BlockSpec((PEER_HEADS, PEER_NB, PEER_IG, tt), lambda i, e: (0, e, 0, i))
    row = pl.BlockSpec((1, d), lambda i, e: (0, 0))
    tok = pl.BlockSpec((tt, d), lambda i, e: (i, 0))
    grouped = lambda a: a.reshape(PEER_HEADS, PEER_N_KEYS // PEER_IG, PEER_IG, t)
    return pl.pallas_call(
        _peer_kernel,
        grid=(t // tt, ne),
        in_specs=[tok, pl.BlockSpec((PEER_E, d), lambda i, e: (e, 0)), pl.BlockSpec((d, PEER_E), lambda i, e: (0, e)),
                  grp, big, grp, big, tok, row, row],
        out_specs=[pl.BlockSpec((tt, d), lambda i, e: (i, 0)), pl.BlockSpec((tt, d), lambda i, e: (i, 0))],
        out_shape=[jax.ShapeDtypeStruct((t, d), F32), jax.ShapeDtypeStruct((t, d), BF16)],
        scratch_shapes=[pltpu.VMEM((d, tt), F32), pltpu.VMEM((PEER_NB, PEER_EB, tt), BF16)],
        compiler_params=_params("parallel", "arbitrary"),
        name="peer_dense",
    )(hb, u_b, vt_b, grouped(rk), cnt, grouped(e1), e2, h, ln_g.reshape(1, d).astype(F32),
      ln_b.reshape(1, d).astype(F32))


def kernel(x, positions, w_in, gate_bias, rel_bias, conv_w, conv_b, lru_w_r, lru_b_r, lru_w_i, lru_b_i, lru_lambda,
           diff_lambda, diff_subln_g, w_branch, w_out, ln1_g, ln1_b, peer_w_q, peer_sub_keys, peer_u, peer_v,
           ln2_g, ln2_b):
    batch, seq, d = x.shape
    t = batch * seq
    h = x.reshape(t, d).astype(F32)
    hb = h.astype(BF16)
    pos_col = positions.reshape(t, 1).astype(jnp.int32)
    inv_freq = jnp.power(ROPE_THETA, -jnp.arange(0, HEAD_DIM, 2, dtype=F32) / HEAD_DIM)
    inv_tiled = jnp.tile(inv_freq, LANES // (HEAD_DIM // 2)).reshape(1, LANES)

    for l in range(DEPTH):
        lam_init = 0.8 - 0.6 * math.exp(-0.3 * l)
        z = _matmul(hb, w_in[l].astype(BF16), BF16)
        ya = _band_attention(z, _band_bias_table(rel_bias[l]), batch, seq)
        yb = _rglru(z, conv_w[l], conv_b[l], lru_w_r[l], lru_b_r[l], lru_w_i[l], lru_b_i[l], lru_lambda[l],
                    batch, seq)
        qr, kr = _rope(z, pos_col, inv_tiled)
        yc = _diff_attention(qr, kr, z, diff_lambda[l], diff_subln_g[l], lam_init, batch, seq)
        h, hb = _merge(ya, yb, yc, z, gate_bias[l], w_branch[l], w_out[l], h, ln1_g[l], ln1_b[l])
        rk, e1, cnt, e2 = _peer_route(hb, peer_w_q[l].T.astype(BF16), peer_sub_keys[l].astype(BF16))
        h, hb = _peer_dense(hb, h, peer_u[l].astype(BF16), peer_v[l].astype(BF16).T, rk, e1, cnt, e2,
                            ln2_g[l], ln2_b[l])
    return h.reshape(batch, seq, d)
```

```python
import functools
import math

import numpy as np
import jax
import jax.numpy as jnp
from jax import lax
from jax.experimental import pallas as pl
from jax.experimental.pallas import tpu as pltpu

D_MODEL = 1024
DEPTH = 2
CHUNK = 64
HEAD_DIM = 64
A_HEADS = 8
A_LOOKBACK = 8
A_MAX_REL = 256
LRU_WIDTH = 512
LRU_BLOCKS = 8
LRU_BLOCK_DIM = 64
CONV_WIDTH = 4
LRU_C = 8.0
C_HEADS = 4
ROPE_THETA = 10000.0
N_BRANCHES = 3
BRANCH_WIDTH = 512
PEER_HEADS = 8
PEER_N_KEYS = 128
PEER_N_EXPERTS = PEER_N_KEYS * PEER_N_KEYS
PEER_HALF = 128
PEER_TOPK = 16
DEEPNORM_ALPHA = (2 * DEPTH) ** 0.25
LN_EPS = 1e-5
NEG_INF = -1e30

LANES = 128
VMEM_LIMIT = 56 * 1024 * 1024

COL_AQ, COL_AK, COL_AV = 0, 512, 1024
COL_BX, COL_BG = 1536, 2048
COL_CQ, COL_CK, COL_CV = 2560, 3072, 3584
COL_GL = 4096
IN_COLS = 7168

F32 = jnp.float32
BF16 = jnp.bfloat16
_NT = (((1,), (1,)), ((), ()))


def _params(*sem):
    return pltpu.CompilerParams(dimension_semantics=sem, vmem_limit_bytes=VMEM_LIMIT)


def _layer_norm(y, g, b):
    mu = jnp.mean(y, axis=-1, keepdims=True)
    d = y - mu
    var = jnp.mean(d * d, axis=-1, keepdims=True)
    return d * lax.rsqrt(var + LN_EPS) * g + b


def _mm_kernel(x_ref, w_ref, o_ref):
    o_ref[...] = jnp.dot(x_ref[...], w_ref[...], preferred_element_type=F32).astype(o_ref.dtype)


def _matmul(x, w, out_dtype, tm=1024, tn=1792):
    m, k = x.shape
    n = w.shape[1]
    return pl.pallas_call(
        _mm_kernel,
        grid=(n // tn, m // tm),
        in_specs=[pl.BlockSpec((tm, k), lambda j, i: (i, 0)),
                  pl.BlockSpec((k, tn), lambda j, i: (0, j))],
        out_specs=pl.BlockSpec((tm, tn), lambda j, i: (i, j)),
        out_shape=jax.ShapeDtypeStruct((m, n), out_dtype),
        compiler_params=_params("parallel", "parallel"),
        name="in_proj",
    )(x, w)


def _rope_kernel(pos_ref, inv_ref, q_ref, k_ref, qo_ref, ko_ref):
    tt = pos_ref.shape[0]
    ang = pos_ref[...].astype(F32) * inv_ref[...]
    c = jnp.cos(ang)
    s = jnp.sin(ang)
    lane = lax.broadcasted_iota(jnp.int32, (tt, LANES), 1)
    first = (lane % HEAD_DIM) < (HEAD_DIM // 2)
    s_signed = jnp.where(first, -s, s)

    def rot(x):
        partner = jnp.where(first, pltpu.roll(x, LANES - HEAD_DIM // 2, 1), pltpu.roll(x, HEAD_DIM // 2, 1))
        return x * c + partner * s_signed

    for blk in range(q_ref.shape[1] // LANES):
        sl = slice(blk * LANES, (blk + 1) * LANES)
        qo_ref[:, sl] = (rot(q_ref[:, sl].astype(F32)) * (LOG2E * HEAD_DIM ** -0.5)).astype(qo_ref.dtype)
        ko_ref[:, sl] = rot(k_ref[:, sl].astype(F32)).astype(ko_ref.dtype)


def _rope(z, pos_col, inv_tiled, tt=512):
    t = z.shape[0]
    w = 512
    return pl.pallas_call(
        _rope_kernel,
        grid=(t // tt,),
        in_specs=[pl.BlockSpec((tt, 1), lambda i: (i, 0)),
                  pl.BlockSpec((1, LANES), lambda i: (0, 0)),
                  pl.BlockSpec((tt, w), lambda i: (i, COL_CQ // w)),
                  pl.BlockSpec((tt, w), lambda i: (i, COL_CK // w))],
        out_specs=[pl.BlockSpec((tt, w), lambda i: (i, 0)),
                   pl.BlockSpec((tt, w), lambda i: (i, 0))],
        out_shape=[jax.ShapeDtypeStruct((t, w), BF16), jax.ShapeDtypeStruct((t, w), BF16)],
        compiler_params=_params("parallel"),
        name="rope",
    )(pos_col, inv_tiled, z, z)


BAND_Q = 2 * CHUNK
BAND_K = (A_LOOKBACK + 2) * CHUNK
BAND_PAD = A_LOOKBACK * CHUNK


def _band_bias_table(rel_bias):
    rb = rel_bias.astype(F32)
    nh, rel = rb.shape
    far = jnp.broadcast_to(rb[:, -1:], (nh, BAND_PAD - A_MAX_REL))
    near = jnp.broadcast_to(rb[:, :1], (nh, BAND_K - (BAND_PAD - A_MAX_REL) - rel))
    wrap = jnp.broadcast_to(rb[:, -1:], (nh, BAND_Q))
    v = jnp.concatenate([far, rb[:, ::-1], near, wrap], axis=1)
    period = BAND_K + BAND_Q
    flat = jnp.tile(v, (1, BAND_Q))[:, :BAND_Q * (period - 1)]
    bias = flat.reshape(nh, BAND_Q, period - 1)[:, :, :BAND_K]
    qi = np.arange(BAND_Q)[:, None]
    kj = np.arange(BAND_K)[None, :]
    q_chunk = qi // CHUNK + A_LOOKBACK
    k_chunk = kj // CHUNK
    allowed = (k_chunk <= q_chunk) & (k_chunk >= q_chunk - A_LOOKBACK)
    return jnp.where(jnp.asarray(allowed)[None], bias, NEG_INF)


def _band_kernel(q_ref, k_ref, v_ref, tab_ref, o_ref, kp_ref, vp_ref):
    c = pl.program_id(2)

    @pl.when(c == 0)
    def _():
        zeros = jnp.zeros((BAND_PAD, LANES), kp_ref.dtype)
        kp_ref[0:BAND_PAD, :] = zeros
        vp_ref[0:BAND_PAD, :] = zeros
        kp_ref[BAND_PAD:, :] = k_ref[...]
        vp_ref[BAND_PAD:, :] = v_ref[...]

    lane = lax.broadcasted_iota(jnp.int32, (BAND_Q, LANES), 1)
    col = lax.broadcasted_iota(jnp.int32, (BAND_Q, BAND_K), 1)
    firsts = [(c * BAND_NBLK + blk) * BAND_Q for blk in range(BAND_NBLK)]
    starts = [pl.multiple_of(f, BAND_Q) for f in firsts]
    kws = [kp_ref[pl.ds(st, BAND_K), :] for st in starts]
    scores = []
    for blk in range(BAND_NBLK):
        q = q_ref[blk * BAND_Q:(blk + 1) * BAND_Q, :]
        for hh in range(2):
            keep = (lane < HEAD_DIM) if hh == 0 else (lane >= HEAD_DIM)
            qm = jnp.where(keep, q, jnp.zeros_like(q))
            scores.append(lax.dot_general(qm, kws[blk], _NT, preferred_element_type=F32))
    probs = []
    for blk in range(BAND_NBLK):
        valid = (col + firsts[blk]) >= BAND_PAD
        for hh in range(2):
            s = scores[2 * blk + hh] * (HEAD_DIM ** -0.5)
            s = jnp.where(valid, s + tab_ref[hh], NEG_INF)
            m = jnp.max(s, axis=-1, keepdims=True)
            e = jnp.exp(s - m)
            probs.append((e / jnp.sum(e, axis=-1, keepdims=True)).astype(kp_ref.dtype))
    for blk in range(BAND_NBLK):
        vw = vp_ref[pl.ds(starts[blk], BAND_K), :]
        outs = [jnp.dot(probs[2 * blk + hh], vw, preferred_element_type=F32) for hh in range(2)]
        o_ref[blk * BAND_Q:(blk + 1) * BAND_Q, :] = jnp.where(lane < HEAD_DIM, outs[0], outs[1]).astype(o_ref.dtype)


BAND_NBLK = 8


def _band_attention(z, table, batch, seq):
    t = z.shape[0]
    nq = seq // (BAND_Q * BAND_NBLK)
    return pl.pallas_call(
        _band_kernel,
        grid=(batch, A_HEADS // 2, nq),
        in_specs=[pl.BlockSpec((BAND_Q * BAND_NBLK, LANES), lambda b, g, c: (b * nq + c, COL_AQ // LANES + g)),
                  pl.BlockSpec((seq, LANES), lambda b, g, c: (b, COL_AK // LANES + g)),
                  pl.BlockSpec((seq, LANES), lambda b, g, c: (b, COL_AV // LANES + g)),
                  pl.BlockSpec((2, BAND_Q, BAND_K), lambda b, g, c: (g, 0, 0))],
        out_specs=pl.BlockSpec((BAND_Q * BAND_NBLK, LANES), lambda b, g, c: (b * nq + c, g)),
        out_shape=jax.ShapeDtypeStruct((t, A_HEADS * HEAD_DIM), BF16),
        scratch_shapes=[pltpu.VMEM((seq + BAND_PAD, LANES), z.dtype),
                        pltpu.VMEM((seq + BAND_PAD, LANES), z.dtype)],
        compiler_params=_params("parallel", "parallel", "arbitrary"),
        name="band_attn",
    )(z, z, z, table)


LRU_ROWS = 8


def _lru_kernel(bx_ref, bg_ref, cw_ref, cb_ref, wr_ref, br_ref, wi_ref, bi_ref, lam_ref, o_ref,
                tail_ref, h_ref, a_ref, u_ref):
    tt = bx_ref.shape[0]

    @pl.when(pl.program_id(1) == 0)
    def _():
        tail_ref[...] = jnp.zeros_like(tail_ref)
        h_ref[...] = jnp.zeros_like(h_ref)

    x = bx_ref[...].astype(F32)
    xx = jnp.concatenate([tail_ref[...], x], axis=0)
    tail_ref[...] = x[tt - 8:, :]
    xc = cb_ref[...] + xx[5:5 + tt, :] * cw_ref[0:1, :]
    for tap in range(1, CONV_WIDTH):
        xc = xc + xx[5 + tap:5 + tap + tt, :] * cw_ref[tap:tap + 1, :]
    xcb = xc.astype(BF16)
    r = jax.nn.sigmoid(jnp.dot(xcb, wr_ref[...], preferred_element_type=F32) + br_ref[...])
    i = jax.nn.sigmoid(jnp.dot(xcb, wi_ref[...], preferred_element_type=F32) + bi_ref[...])
    log_a = LRU_C * r * jax.nn.log_sigmoid(lam_ref[...])
    a = jnp.exp(log_a)
    a_ref[...] = a
    u_ref[...] = jnp.sqrt(1.0 - a * a) * (i * xc)

    def group(gi, h):
        base = pl.multiple_of(gi * LRU_ROWS, LRU_ROWS)
        a = a_ref[pl.ds(base, LRU_ROWS), :]
        u = u_ref[pl.ds(base, LRU_ROWS), :]
        row = lax.broadcasted_iota(jnp.int32, a.shape, 0)
        for d in (1, 2, 4):
            a_prev = jnp.where(row >= d, pltpu.roll(a, d, 0), 1.0)
            u_prev = jnp.where(row >= d, pltpu.roll(u, d, 0), 0.0)
            u = a * u_prev + u
            a = a * a_prev
        hh = u + a * h
        u_ref[pl.ds(base, LRU_ROWS), :] = hh
        return hh[LRU_ROWS - 1:LRU_ROWS, :]

    h_ref[...] = lax.fori_loop(0, tt // LRU_ROWS, group, h_ref[...])
    o_ref[...] = (u_ref[...] * jax.nn.gelu(bg_ref[...].astype(F32))).astype(o_ref.dtype)


def _block_diag(w):
    eye = jnp.eye(LRU_BLOCKS, dtype=w.dtype)
    return jnp.einsum('gij,gh->gihj', w, eye).reshape(LRU_WIDTH, LRU_WIDTH)


def _rglru(z, conv_w, conv_b, w_r, b_r, w_i, b_i, lru_lambda, batch, seq, tt=512):
    t = z.shape[0]
    nt = seq // tt
    w = LRU_WIDTH
    row = lambda a: a.reshape(1, w).astype(F32)
    full = lambda shape: pl.BlockSpec(shape, lambda b, j: (0,) * len(shape))
    return pl.pallas_call(
        _lru_kernel,
        grid=(batch, nt),
        in_specs=[pl.BlockSpec((tt, w), lambda b, j: (b * nt + j, COL_BX // w)),
                  pl.BlockSpec((tt, w), lambda b, j: (b * nt + j, COL_BG // w)),
                  full((CONV_WIDTH, w)), full((1, w)), full((w, w)), full((1, w)), full((w, w)), full((1, w)),
                  full((1, w))],
        out_specs=pl.BlockSpec((tt, w), lambda b, j: (b * nt + j, 0)),
        out_shape=jax.ShapeDtypeStruct((t, w), BF16),
        scratch_shapes=[pltpu.VMEM((8, w), F32), pltpu.VMEM((1, w), F32),
                        pltpu.VMEM((tt, w), F32), pltpu.VMEM((tt, w), F32)],
        compiler_params=_params("parallel", "arbitrary"),
        name="rglru",
    )(z, z, conv_w.astype(F32), row(conv_b), _block_diag(w_r).astype(BF16), row(b_r),
      _block_diag(w_i).astype(BF16), row(b_i), row(lru_lambda))


DIFF_TQ = 1024
DIFF_TK = 1024
LOG2E = 1.4426950408889634
DIFF_NSUB = 4


def _diff_kernel(q_ref, k_ref, v_ref, dl_ref, g_ref, o_ref, m_ref, l_ref, acc_ref, *, lam_init):
    qi = pl.program_id(2)
    tq, tk = DIFF_TQ, DIFF_TK
    q = q_ref[...]
    lane = lax.broadcasted_iota(jnp.int32, (tq, LANES), 1)
    qs = (jnp.where(lane < HEAD_DIM, q, jnp.zeros_like(q)), jnp.where(lane >= HEAD_DIM, q, jnp.zeros_like(q)))
    m_ref[...] = jnp.full(m_ref.shape, NEG_INF, F32)
    l_ref[...] = jnp.zeros(l_ref.shape, F32)
    acc_ref[...] = jnp.zeros(acc_ref.shape, F32)

    chains = [(sub, mp) for sub in range(DIFF_NSUB) for mp in range(2)]
    tsub = tq // DIFF_NSUB

    def blocks(js, masked):
        staged = []
        for j in js:
            start = pl.multiple_of(j * tk, tk)
            kb = k_ref[pl.ds(start, tk), :]
            scores = [lax.dot_general(kb, qs[mp][sub * tsub:(sub + 1) * tsub], _NT, preferred_element_type=F32)
                      for sub, mp in chains]
            staged.append((v_ref[j], scores))
        for idx, (vtb, scores) in enumerate(staged):
            diagonal = masked and idx == len(staged) - 1
            weights = []
            for (sub, mp), s in zip(chains, scores):
                cols = slice(sub * tsub, (sub + 1) * tsub)
                if diagonal:
                    key = lax.broadcasted_iota(jnp.int32, (tk, tsub), 0)
                    qry = lax.broadcasted_iota(jnp.int32, (tk, tsub), 1) + sub * tsub
                    s = jnp.where((key // CHUNK) <= (qry // CHUNK), s, NEG_INF)
                m_old = m_ref[mp, :, cols]
                m_new = jnp.maximum(m_old, jnp.max(s, axis=0, keepdims=True))
                alpha = jnp.exp2(m_old - m_new)
                p = jnp.exp2(s - m_new)
                l_ref[mp, :, cols] = alpha * l_ref[mp, :, cols] + jnp.sum(p, axis=0, keepdims=True)
                m_ref[mp, :, cols] = m_new
                weights.append((alpha, p.astype(vtb.dtype)))
            for (sub, mp), (alpha, p) in zip(chains, weights):
                cols = slice(sub * tsub, (sub + 1) * tsub)
                acc_ref[mp, :, cols] = alpha * acc_ref[mp, :, cols] + jnp.dot(vtb, p, preferred_element_type=F32)

    def pair(i, carry):
        blocks([2 * i, 2 * i + 1], False)
        return carry

    lax.fori_loop(0, qi // 2, pair, 0)

    @pl.when(qi % 2 == 1)
    def _():
        blocks([qi - 1, qi], True)

    @pl.when(qi % 2 == 0)
    def _():
        blocks([qi], True)

    dl = dl_ref[...]
    lam = (jnp.exp(jnp.sum(dl[0:1] * dl[1:2], axis=-1, keepdims=True))
           - jnp.exp(jnp.sum(dl[2:3] * dl[3:4], axis=-1, keepdims=True)) + lam_init)
    o = acc_ref[0] / l_ref[0] - lam * (acc_ref[1] / l_ref[1])
    ms = jnp.mean(o * o, axis=0, keepdims=True)
    o = o * lax.rsqrt(ms + LN_EPS) * g_ref[...] * (1.0 - lam_init)
    o_ref[...] = o.T.astype(o_ref.dtype)


def _diff_attention(qr, kr, z, diff_lambda, subln_g, lam_init, batch, seq):
    t = z.shape[0]
    nq = seq // DIFF_TQ
    nk = seq // DIFF_TK
    hd = 2 * HEAD_DIM
    vt = z[:, COL_CV:COL_CV + C_HEADS * hd].reshape(t // DIFF_TK, DIFF_TK, C_HEADS * hd).transpose(0, 2, 1)
    return pl.pallas_call(
        functools.partial(_diff_kernel, lam_init=lam_init),
        grid=(batch, C_HEADS, nq),
        in_specs=[pl.BlockSpec((DIFF_TQ, LANES), lambda b, h, i: (b * nq + i, h)),
                  pl.BlockSpec((seq, LANES), lambda b, h, i: (b, h)),
                  pl.BlockSpec((nk, hd, DIFF_TK), lambda b, h, i: (b, h, 0)),
                  pl.BlockSpec((4, HEAD_DIM), lambda b, h, i: (0, 0)),
                  pl.BlockSpec((hd, 1), lambda b, h, i: (0, 0))],
        out_specs=pl.BlockSpec((DIFF_TQ, LANES), lambda b, h, i: (b * nq + i, h)),
        out_shape=jax.ShapeDtypeStruct((t, C_HEADS * hd), BF16),
        scratch_shapes=[pltpu.VMEM((2, 1, DIFF_TQ), F32), pltpu.VMEM((2, 1, DIFF_TQ), F32),
                        pltpu.VMEM((2, hd, DIFF_TQ), F32)],
        compiler_params=_params("parallel", "parallel", "arbitrary"),
        name="diff_attn",
    )(qr, kr, vt, diff_lambda.astype(F32), subln_g.reshape(hd, 1).astype(F32))


def _merge_kernel(ya_ref, yb_ref, yc_ref, gl0_ref, gl1_ref, gl2_ref, gb_ref, wb_ref, wo_ref, h_ref, g_ref, b_ref,
                  o_ref, ob_ref):
    merged = None
    for n, (y_ref, gl_ref) in enumerate(((ya_ref, gl0_ref), (yb_ref, gl1_ref), (yc_ref, gl2_ref))):
        proj = jnp.dot(y_ref[...], wb_ref[n], preferred_element_type=F32)
        gate = jax.nn.sigmoid(gl_ref[...].astype(F32) + gb_ref[n:n + 1, :])
        merged = gate * proj if merged is None else merged + gate * proj
    mix = jnp.dot(merged.astype(BF16), wo_ref[...], preferred_element_type=F32)
    y = _layer_norm(DEEPNORM_ALPHA * h_ref[...] + mix, g_ref[...], b_ref[...])
    o_ref[...] = y
    ob_ref[...] = y.astype(ob_ref.dtype)


def _merge(ya, yb, yc, z, gate_bias, w_branch, w_out, h, ln_g, ln_b, tt=512):
    t = h.shape[0]
    d = D_MODEL
    bw = BRANCH_WIDTH
    ytile = pl.BlockSpec((tt, bw), lambda i: (i, 0))
    gl = lambda n: pl.BlockSpec((tt, d), lambda i: (i, COL_GL // d + n))
    full = lambda shape: pl.BlockSpec(shape, lambda i: (0,) * len(shape))
    return pl.pallas_call(
        _merge_kernel,
        grid=(t // tt,),
        in_specs=[ytile, ytile, ytile, gl(0), gl(1), gl(2), full((N_BRANCHES, d)), full((N_BRANCHES, bw, d)),
                  full((d, d)), pl.BlockSpec((tt, d), lambda i: (i, 0)), full((1, d)), full((1, d))],
        out_specs=[pl.BlockSpec((tt, d), lambda i: (i, 0)), pl.BlockSpec((tt, d), lambda i: (i, 0))],
        out_shape=[jax.ShapeDtypeStruct((t, d), F32), jax.ShapeDtypeStruct((t, d), BF16)],
        compiler_params=_params("parallel"),
        name="merge_out_ln",
    )(ya, yb, yc, z, z, z, gate_bias.astype(F32), w_branch.astype(BF16), w_out.astype(BF16), h,
      ln_g.reshape(1, d).astype(F32), ln_b.reshape(1, d).astype(F32))


def _top_rows(s, k, with_rank=False):
    rows = []
    cur = s
    rank = jnp.full(s.shape, float(k), F32) if with_rank else None
    for r in range(k):
        m = jnp.max(cur, axis=0, keepdims=True)
        rows.append(m)
        hit = cur == m
        if with_rank:
            rank = jnp.where(hit, float(r), rank)
        cur = jnp.where(hit, -jnp.inf, cur)
    return (rows, rank) if with_rank else rows


def _batcher_pairs(n):
    pairs = []
    p = 1
    while p < n:
        k = p
        while k >= 1:
            for j in range(k % p, n - k, 2 * k):
                for i in range(min(k, n - j - k)):
                    if (i + j) // (2 * p) == (i + j + k) // (2 * p):
                        pairs.append((i + j, i + j + k))
            k //= 2
        p *= 2
    return pairs


def _sorted_top(s, k):
    sub = s.shape[0] // k
    x = [s[i * sub:(i + 1) * sub, :] for i in range(k)]

    def exchange(lo, hi):
        x[lo], x[hi] = jnp.maximum(x[lo], x[hi]), jnp.minimum(x[lo], x[hi])

    for lo, hi in _batcher_pairs(k):
        exchange(lo, hi)
    shift = sub // 2
    while shift >= 1:
        y = [pltpu.roll(v, shift, 0) for v in x]
        x = [jnp.maximum(x[i], y[k - 1 - i]) for i in range(k)]
        stride = k // 2
        while stride >= 1:
            for i in range(k):
                if i & stride == 0:
                    exchange(i, i + stride)
            stride //= 2
        shift //= 2
    return x


def _count_prefix(a, pred):
    c8 = pred(a[7])
    c4 = pred(jnp.where(c8, a[11], a[3]))
    c2 = pred(jnp.where(c8, jnp.where(c4, a[13], a[9]), jnp.where(c4, a[5], a[1])))
    hi = jnp.where(c4, jnp.where(c2, a[14], a[12]), jnp.where(c2, a[10], a[8]))
    lo = jnp.where(c4, jnp.where(c2, a[6], a[4]), jnp.where(c2, a[2], a[0]))
    c1 = pred(jnp.where(c8, hi, lo))
    c0 = pred(a[15])
    one = lambda c, v: jnp.where(c, v, 0.0)
    return one(c8, 8.0) + one(c4, 4.0) + one(c2, 2.0) + one(c1, 1.0) + one(c0, 1.0)


def _pair_word(x):
    bits = lax.bitcast_convert_type(x.astype(BF16).astype(F32), jnp.uint32)
    return bits | (bits >> 16)


def _route_kernel(h_ref, wq_ref, keys_ref, rk_ref, e1_ref, cnt_ref, e2_ref):
    qt = lax.dot_general(wq_ref[...], h_ref[...], _NT, preferred_element_type=F32).astype(BF16)
    for h in range(PEER_HEADS):
        sc = []
        for p in range(2):
            r0 = (h * 2 + p) * PEER_HALF
            sc.append(jnp.dot(keys_ref[h, p], qt[r0:r0 + PEER_HALF, :], preferred_element_type=F32))
        s1, s2 = sc
        a8 = _sorted_top(s1, PEER_TOPK)
        b8 = _sorted_top(s2, PEER_TOPK)
        a = [v[0:1, :] for v in a8]
        b = [v[0:1, :] for v in b8]
        cand = jnp.concatenate([a[k] + b[l] for k in range(PEER_TOPK) for l in range(PEER_TOPK // (k + 1))], axis=0)
        tau = _top_rows(cand, PEER_TOPK)[PEER_TOPK - 1]
        top = a[0] + b[0]
        zsum = jnp.sum(jnp.where(cand >= tau, jnp.exp(cand - top), 0.0), axis=0, keepdims=True)
        tau8 = jnp.broadcast_to(tau, a8[0].shape)
        sub = a8[0].shape[0]
        slabs = lambda s: [s[i * sub:(i + 1) * sub, :] for i in range(s.shape[0] // sub)]
        rank1 = jnp.concatenate([_count_prefix(a8, lambda v: v > x) for x in slabs(s1)], axis=0)
        cnt = jnp.concatenate([_count_prefix(a8, lambda v: v + x >= tau8) for x in slabs(s2)], axis=0)
        rk_ref[h] = _pair_word(rank1)
        e1_ref[h] = _pair_word(jnp.exp(s1 - a[0]) / zsum)
        cnt_ref[h] = cnt.astype(cnt_ref.dtype)
        e2_ref[h] = jnp.exp(s2 - b[0]).astype(e2_ref.dtype)


def _peer_route(hb, wq_t, keys, tt=256):
    t = hb.shape[0]
    shape = (PEER_HEADS, PEER_N_KEYS, t)
    bspec = pl.BlockSpec((PEER_HEADS, PEER_N_KEYS, tt), lambda i: (0, 0, i))
    return pl.pallas_call(
        _route_kernel,
        grid=(t // tt,),
        in_specs=[pl.BlockSpec((tt, D_MODEL), lambda i: (i, 0)),
                  pl.BlockSpec(wq_t.shape, lambda i: (0, 0)),
                  pl.BlockSpec(keys.shape, lambda i: (0, 0, 0, 0))],
        out_specs=[bspec, bspec, bspec, bspec],
        out_shape=[jax.ShapeDtypeStruct(shape, jnp.uint32), jax.ShapeDtypeStruct(shape, jnp.uint32),
                   jax.ShapeDtypeStruct(shape, BF16), jax.ShapeDtypeStruct(shape, BF16)],
        compiler_params=_params("parallel"),
        name="peer_route",
    )(hb, wq_t, keys)


PEER_TT = 512
PEER_IG = 8
PEER_EB = PEER_IG * PEER_N_KEYS
PEER_NB = 2
PEER_E = PEER_NB * PEER_EB
PEER_JB = 64
GELU_C1 = -2.0 * math.sqrt(2.0 / math.pi) * LOG2E
GELU_C2 = GELU_C1 * 0.044715


def _gelu_tanh(x):
    return x * (1.0 / (1.0 + jnp.exp2(x * (GELU_C1 + GELU_C2 * (x * x)))))


def _peer_kernel(hb_ref, u_ref, vt_ref, rk_ref, cnt_ref, e1_ref, e2_ref, h_ref, g_ref, b_ref,
                 o_ref, ob_ref, acc_ref, p_ref):
    e = pl.program_id(1)
    tt = hb_ref.shape[0]

    @pl.when(e == 0)
    def _():
        acc_ref[...] = jnp.zeros_like(acc_ref)

    def act_rows(k, r0, r1):
        a = lax.dot_general(u_ref[k * PEER_EB + r0:k * PEER_EB + r1, :], hb_ref[...], _NT,
                            preferred_element_type=F32)
        p_ref[k, r0:r1, :] = _gelu_tanh(a.astype(p_ref.dtype))

    def row_pairs(ref, h, k, ii, ls):
        return pltpu.bitcast(jnp.broadcast_to(ref[h, k, ii:ii + 1, ls], (PEER_JB // 2, LANES)), BF16)

    def gate_cols(k, c0, c1):
        for lc in range(c0 // LANES, c1 // LANES):
            ls = slice(lc * LANES, (lc + 1) * LANES)
            for jb in range(PEER_N_KEYS // PEER_JB):
                js = slice(jb * PEER_JB, (jb + 1) * PEER_JB)
                w = [jnp.zeros((PEER_JB, LANES), BF16) for _ in range(PEER_IG)]
                for h in range(PEER_HEADS):
                    cnt = cnt_ref[h, js, ls]
                    e2 = e2_ref[h, js, ls]
                    for ii in range(PEER_IG):
                        gate = row_pairs(e1_ref, h, k, ii, ls) * e2
                        w[ii] = jnp.where(row_pairs(rk_ref, h, k, ii, ls) < cnt, w[ii] + gate, w[ii])
                for ii in range(PEER_IG):
                    rs = slice(ii * PEER_N_KEYS + jb * PEER_JB, ii * PEER_N_KEYS + (jb + 1) * PEER_JB)
                    p_ref[k, rs, ls] = w[ii] * p_ref[k, rs, ls]

    def out_cols(k, c0, c1):
        acc_ref[:, c0:c1] += jnp.dot(vt_ref[:, k * PEER_EB:(k + 1) * PEER_EB], p_ref[k, :, c0:c1],
                                     preferred_element_type=F32)

    act_rows(0, 0, PEER_EB)
    for k in range(PEER_NB):
        if k + 1 < PEER_NB:
            act_rows(k + 1, 0, PEER_EB)
        gate_cols(k, 0, tt)
        out_cols(k, 0, tt)

    @pl.when(e == pl.num_programs(1) - 1)
    def _():
        ff = acc_ref[...].T
        y = _layer_norm(DEEPNORM_ALPHA * h_ref[...] + ff, g_ref[...], b_ref[...])
        o_ref[...] = y
        ob_ref[...] = y.astype(ob_ref.dtype)


def _peer_dense(hb, h, u_b, vt_b, rk, e1, cnt, e2, ln_g, ln_b):
    t = h.shape[0]
    d = D_MODEL
    tt = min(PEER_TT, t)
    ne = PEER_N_EXPERTS // PEER_E
    big = pl.BlockSpec((PEER_HEADS, PEER_N_KEYS, tt), lambda i, e: (0, 0, i))
    grp = pl.BlockSpec((PEER_HEADS, PEER_NB, PEER_IG, tt), lambda i, e: (0, e, 0, i))
    row = pl.BlockSpec((1, d), lambda i, e: (0, 0))
    tok = pl.BlockSpec((tt, d), lambda i, e: (i, 0))
    grouped = lambda a: a.reshape(PEER_HEADS, PEER_N_KEYS // PEER_IG, PEER_IG, t)
    return pl.pallas_call(
        _peer_kernel,
        grid=(t // tt, ne),
        in_specs=[tok, pl.BlockSpec((PEER_E, d), lambda i, e: (e, 0)), pl.BlockSpec((d, PEER_E), lambda i, e: (0, e)),
                  grp, big, grp, big, tok, row, row],
        out_specs=[pl.BlockSpec((tt, d), lambda i, e: (i, 0)), pl.BlockSpec((tt, d), lambda i, e: (i, 0))],
        out_shape=[jax.ShapeDtypeStruct((t, d), F32), jax.ShapeDtypeStruct((t, d), BF16)],
        scratch_shapes=[pltpu.VMEM((d, tt), F32), pltpu.VMEM((PEER_NB, PEER_EB, tt), BF16)],
        compiler_params=_params("parallel", "arbitrary"),
        name="peer_dense",
    )(hb, u_b, vt_b, grouped(rk), cnt, grouped(e1), e2, h, ln_g.reshape(1, d).astype(F32),
      ln_b.reshape(1, d).astype(F32))


def kernel(x, positions, w_in, gate_bias, rel_bias, conv_w, conv_b, lru_w_r, lru_b_r, lru_w_i, lru_b_i, lru_lambda,
           diff_lambda, diff_subln_g, w_branch, w_out, ln1_g, ln1_b, peer_w_q, peer_sub_keys, peer_u, peer_v,
           ln2_g, ln2_b):
    batch, seq, d = x.shape
    t = batch * seq
    h = x.reshape(t, d).astype(F32)
    hb = h.astype(BF16)
    pos_col = positions.reshape(t, 1).astype(jnp.int32)
    inv_freq = jnp.power(ROPE_THETA, -jnp.arange(0, HEAD_DIM, 2, dtype=F32) / HEAD_DIM)
    inv_tiled = jnp.tile(inv_freq, LANES // (HEAD_DIM // 2)).reshape(1, LANES)

    for l in range(DEPTH):
        lam_init = 0.8 - 0.6 * math.exp(-0.3 * l)
        z = _matmul(hb, w_in[l].astype(BF16), BF16)
        ya = _band_attention(z, _band_bias_table(rel_bias[l]), batch, seq)
        yb = _rglru(z, conv_w[l], conv_b[l], lru_w_r[l], lru_b_r[l], lru_w_i[l], lru_b_i[l], lru_lambda[l],
                    batch, seq)
        qr, kr = _rope(z, pos_col, inv_tiled)
        yc = _diff_attention(qr, kr, z, diff_lambda[l], diff_subln_g[l], lam_init, batch, seq)
        h, hb = _merge(ya, yb, yc, z, gate_bias[l], w_branch[l], w_out[l], h, ln1_g[l], ln1_b[l])
        rk, e1, cnt, e2 = _peer_route(hb, peer_w_q[l].T.astype(BF16), peer_sub_keys[l].astype(BF16))
        h, hb = _peer_dense(hb, h, peer_u[l].astype(BF16), peer_v[l].astype(BF16).T, rk, e1, cnt, e2,
                            ln2_g[l], ln2_b[l])
    return h.reshape(batch, seq, d)
```
